```python
import math
import jax
import jax.numpy as jnp
from jax import lax
import numpy as np

D_MODEL = 1024
BATCH = 16
SEQ = 2048
DEPTH = 4

GRID_W = 64
CTX_LEN = 256
EPS = 1e-6
N_MOD = 9
D_FF = 2816
FFN_RES = 0.5

SSD_HEADS = 16
SSD_HEAD_DIM = 64
SSD_INNER = SSD_HEADS * SSD_HEAD_DIM
SSD_GROUPS = 4
SSD_STATE = 128
SSD_CONV = 4
SSD_CHUNK = 128
SSD_XBC = SSD_INNER + 2 * SSD_GROUPS * SSD_STATE

MLA_HEADS = 8
MLA_Q_RANK = 384
MLA_KV_RANK = 256
MLA_NOPE = 128
MLA_ROPE = 64
MLA_V = 128
MLA_QK = MLA_NOPE + MLA_ROPE
ROPE_THETA = 10000.0
ATTN_BLOCK = 128

POOL_WINDOWS = (2, 4, 8, 16)
POOL_GROUP = 256
POOL_WIDTH = POOL_GROUP * len(POOL_WINDOWS)

CONV_WIDTH = 1024
CONV_K = 3

N_BRANCH = 4
IN_SPLITS = (SSD_INNER, SSD_XBC, 2 * SSD_HEADS, MLA_Q_RANK, MLA_KV_RANK, MLA_ROPE,
             POOL_WIDTH, 3 * CONV_WIDTH, N_BRANCH * D_MODEL)
IN_SPLIT_IDX = tuple(int(v) for v in np.cumsum(IN_SPLITS)[:-1])
D_IN = int(sum(IN_SPLITS))

kernel_name = 'hybrid_ssd_mla_pool_conv_prefix_dit'


def rms_norm(x, w):
    xf = x.astype(jnp.float32)
    y = xf * lax.rsqrt(jnp.mean(xf * xf, axis=-1, keepdims=True) + EPS)
    return (y * w.astype(jnp.float32)).astype(x.dtype)


def modulate(h, shift, scale):
    return h * (1 + scale) + shift


def swiglu(h, w_gate, w_up, w_down):
    return (jax.nn.silu(h @ w_gate) * (h @ w_up)) @ w_down


def depthwise_conv(x, w):
    k = w.shape[0]
    pad_l = k // 2
    return lax.conv_general_dilated(
        x, w[:, None, :].astype(x.dtype), window_strides=(1,),
        padding=[(pad_l, k - 1 - pad_l)],
        dimension_numbers=('NWC', 'WIO', 'NWC'),
        feature_group_count=x.shape[-1])


def seq_flip(t, reverse):
    return jnp.flip(t, axis=1) if reverse else t


def axial_rope_tables(n_tokens):
    rows = n_tokens // GRID_W
    row = jnp.repeat(jnp.arange(rows), GRID_W).astype(jnp.float32)
    col = jnp.tile(jnp.arange(GRID_W), rows).astype(jnp.float32)
    half = MLA_ROPE // 2
    inv = 1.0 / (ROPE_THETA ** (jnp.arange(0, half, 2, dtype=jnp.float32) / half))
    ar = row[:, None] * inv
    ac = col[:, None] * inv
    ang = jnp.concatenate([ar, ar, ac, ac], axis=-1)
    return jnp.cos(ang), jnp.sin(ang)


def apply_axial_rope(x, cos, sin):
    x1, x2, x3, x4 = jnp.split(x, 4, axis=-1)
    rot = jnp.concatenate([-x2, x1, -x4, x3], axis=-1)
    return (x.astype(jnp.float32) * cos + rot.astype(jnp.float32) * sin).astype(x.dtype)


def ssd_chunked(x, dt, a, b_in, c_in, init_state):
    bsz, n, h, p = x.shape
    g, ns = b_in.shape[-2:]
    r = h // g
    q = SSD_CHUNK
    nc = n // q
    xdt = (x.astype(jnp.float32) * dt[..., None]).reshape(bsz, nc, q, g, r, p)
    bc = b_in.astype(jnp.float32).reshape(bsz, nc, q, g, ns)
    cc = c_in.astype(jnp.float32).reshape(bsz, nc, q, g, ns)
    da = jnp.moveaxis((dt * a).reshape(bsz, nc, q, g, r), 2, -1)
    da_cs = jnp.cumsum(da, axis=-1)
    causal = jnp.tril(jnp.ones((q, q), dtype=bool))
    seg = da_cs[..., :, None] - da_cs[..., None, :]
    decay_in = jnp.exp(jnp.where(causal, seg, -jnp.inf))
    cb = jnp.einsum('bclgn,bcsgn->bcgls', cc, bc)
    y_diag = jnp.einsum('bcgrls,bcsgrp->bclgrp', cb[:, :, :, None] * decay_in, xdt)
    decay_to_end = jnp.exp(da_cs[..., -1:] - da_cs)
    chunk_states = jnp.einsum('bcsgn,bcgrs,bcsgrp->bcgrpn', bc, decay_to_end, xdt)

    def step(s, inp):
        chunk_decay, st = inp
        return s * jnp.exp(chunk_decay)[..., None, None] + st, s

    s0 = init_state.astype(jnp.float32).reshape(bsz, g, r, p, ns)
    s_final, s_prev = lax.scan(step, s0, (jnp.moveaxis(da_cs[..., -1], 1, 0),
                                          jnp.moveaxis(chunk_states, 1, 0)))
    s_prev = jnp.moveaxis(s_prev, 0, 1)
    y_off = jnp.einsum('bclgn,bcgrpn,bcgrl->bclgrp', cc, s_prev, jnp.exp(da_cs))
    return (y_diag + y_off).reshape(bsz, n, h, p), s_final.reshape(bsz, h, p, ns)


def ssd_inputs(xbc, dt_raw, conv_w, conv_b):
    xbc = jax.nn.silu(depthwise_conv(xbc, conv_w) + conv_b)
    bsz, n = xbc.shape[:2]
    xs, b_in, c_in = jnp.split(xbc, [SSD_INNER, SSD_INNER + SSD_GROUPS * SSD_STATE], axis=-1)
    return (xs.reshape(bsz, n, SSD_HEADS, SSD_HEAD_DIM),
            b_in.reshape(bsz, n, SSD_GROUPS, SSD_STATE),
            c_in.reshape(bsz, n, SSD_GROUPS, SSD_STATE),
            dt_raw.reshape(bsz, n, 2, SSD_HEADS).astype(jnp.float32))


def ssd_mixer(z_c, xbc_c, dt_c, z_l, xbc_l, dt_l, conv_w, conv_b, dt_bias, a_log, d_skip, norm_w, w_out):
    xs_c, b_c, c_c, dtr_c = ssd_inputs(xbc_c, dt_c, conv_w, conv_b)
    xs_l, b_l, c_l, dtr_l = ssd_inputs(xbc_l, dt_l, conv_w, conv_b)
    d_f = d_skip.astype(jnp.float32)[:, None]
    y_c = xs_c.astype(jnp.float32) * d_f
    y_l = xs_l.astype(jnp.float32) * d_f
    zero_state = jnp.zeros((xs_c.shape[0], SSD_HEADS, SSD_HEAD_DIM, SSD_STATE), jnp.float32)
    for direction in range(2):
        rev = direction == 1
        a = -jnp.exp(a_log[direction].astype(jnp.float32))
        bias = dt_bias[direction].astype(jnp.float32)
        dtc = jax.nn.softplus(dtr_c[:, :, direction] + bias)
        dtl = jax.nn.softplus(dtr_l[:, :, direction] + bias)
        yc, s_ctx = ssd_chunked(seq_flip(xs_c, rev), seq_flip(dtc, rev), a,
                                seq_flip(b_c, rev), seq_flip(c_c, rev), zero_state)
        yl, _ = ssd_chunked(seq_flip(xs_l, rev), seq_flip(dtl, rev), a,
                            seq_flip(b_l, rev), seq_flip(c_l, rev), s_ctx)
        y_c = y_c + seq_flip(yc, rev)
        y_l = y_l + seq_flip(yl, rev)
    yc = y_c.reshape(z_c.shape).astype(z_c.dtype) * jax.nn.silu(z_c)
    yl = y_l.reshape(z_l.shape).astype(z_l.dtype) * jax.nn.silu(z_l)
    return rms_norm(yc, norm_w) @ w_out, rms_norm(yl, norm_w) @ w_out


def mla_qkv(q_down, kv_down, k_rope, q_norm, w_uq, kv_norm, w_uk, w_uv, cos, sin):
    bsz, n = q_down.shape[:2]
    q = (rms_norm(q_down, q_norm) @ w_uq).reshape(bsz, n, MLA_HEADS, MLA_QK)
    c_kv = rms_norm(kv_down, kv_norm)
    k_nope = (c_kv @ w_uk).reshape(bsz, n, MLA_HEADS, MLA_NOPE)
    v = (c_kv @ w_uv).reshape(bsz, n, MLA_HEADS, MLA_V)
    q_nope, q_pe = jnp.split(q, [MLA_NOPE], axis=-1)
    if cos is not None:
        q_pe = apply_axial_rope(q_pe, cos[:, None, :], sin[:, None, :])
        k_rope = apply_axial_rope(k_rope, cos, sin)
    k_pe = jnp.broadcast_to(k_rope[:, :, None, :], (bsz, n, MLA_HEADS, MLA_ROPE))
    q = jnp.concatenate([q_nope, q_pe], axis=-1)
    k = jnp.concatenate([k_nope, k_pe], axis=-1)
    return q, k, v


def attend_blocks(q, k, v):
    bsz, lq, h, dk = q.shape
    nb = lq // ATTN_BLOCK
    qb = jnp.moveaxis(q.reshape(bsz, nb, ATTN_BLOCK, h, dk), 1, 0)
    scale = dk ** -0.5

    def one_block(q_blk):
        s = jnp.einsum('bqhd,bkhd->bhqk', q_blk, k).astype(jnp.float32) * scale
        p = jax.nn.softmax(s, axis=-1).astype(v.dtype)
        return jnp.einsum('bhqk,bkhd->bqhd', p, v)

    o = lax.map(one_block, qb)
    return jnp.moveaxis(o, 0, 1).reshape(bsz, lq, h, v.shape[-1])


def multiscale_pool(v):
    n = v.shape[1]
    t = jnp.arange(n)
    vf = v.astype(jnp.float32).reshape(v.shape[0], n, len(POOL_WINDOWS), POOL_GROUP)
    cs = jnp.concatenate([jnp.zeros_like(vf[:, :1]), jnp.cumsum(vf, axis=1)], axis=1)
    outs = []
    for gi, win in enumerate(POOL_WINDOWS):
        left = win // 2
        lo = jnp.clip(t - left, 0, n)
        hi = jnp.clip(t - left + win, 0, n)
        cnt = (hi - lo).astype(jnp.float32)[None, :, None]
        outs.append((cs[:, hi, gi] - cs[:, lo, gi]) / cnt - vf[:, :, gi])
    return jnp.stack(outs, axis=2).astype(v.dtype)


def pool_mixer(v, w_grp, scale, w_out):
    p = multiscale_pool(v)
    y = jnp.einsum('blgc,gcd->blgd', p, w_grp).reshape(v.shape) * scale
    return y @ w_out


def short_conv_mixer(u, conv_w, w_out):
    gate_b, gate_c, xin = jnp.split(u, 3, axis=-1)
    return (gate_b * depthwise_conv(gate_c * xin, conv_w)) @ w_out


def gated_merge(gate_logits, branches):
    g = jax.nn.sigmoid(gate_logits.astype(jnp.float32)).astype(gate_logits.dtype)
    g = g.reshape(gate_logits.shape[:-1] + (N_BRANCH, D_MODEL))
    out = g[..., 0, :] * branches[0]
    for k in range(1, N_BRANCH):
        out = out + g[..., k, :] * branches[k]
    return out


def token_mix(a_c, a_l, cos, sin, w_in, ssd_conv_w, ssd_conv_b, ssd_dt_bias, ssd_a_log, ssd_d,
              ssd_norm, ssd_w_out, mla_q_norm, mla_w_uq, mla_kv_norm, mla_w_uk, mla_w_uv,
              mla_w_out, pool_w, pool_scale, pool_w_out, sconv_w, sconv_w_out, w_o):
    (z_c, xbc_c, dt_c, qd_c, kvd_c, kr_c, pv_c, cv_c, g_c) = jnp.split(a_c @ w_in, IN_SPLIT_IDX, axis=-1)
    (z_l, xbc_l, dt_l, qd_l, kvd_l, kr_l, pv_l, cv_l, g_l) = jnp.split(a_l @ w_in, IN_SPLIT_IDX, axis=-1)
    bsz, n_ctx = a_c.shape[:2]
    n_lat = a_l.shape[1]

    ssd_c, ssd_l = ssd_mixer(z_c, xbc_c, dt_c, z_l, xbc_l, dt_l, ssd_conv_w, ssd_conv_b,
                             ssd_dt_bias, ssd_a_log, ssd_d, ssd_norm, ssd_w_out)

    q_c, k_c, v_c = mla_qkv(qd_c, kvd_c, kr_c, mla_q_norm, mla_w_uq, mla_kv_norm, mla_w_uk, mla_w_uv, None, None)
    q_l, k_l, v_l = mla_qkv(qd_l, kvd_l, kr_l, mla_q_norm, mla_w_uq, mla_kv_norm, mla_w_uk, mla_w_uv, cos, sin)
    att_c = attend_blocks(q_c, k_c, v_c)
    att_l = attend_blocks(q_l, jnp.concatenate([k_c, k_l], axis=1), jnp.concatenate([v_c, v_l], axis=1))
    mla_c = att_c.reshape(bsz, n_ctx, MLA_HEADS * MLA_V) @ mla_w_out
    mla_l = att_l.reshape(bsz, n_lat, MLA_HEADS * MLA_V) @ mla_w_out

    pool_c = pool_mixer(pv_c, pool_w, pool_scale, pool_w_out)
    pool_l = pool_mixer(pv_l, pool_w, pool_scale, pool_w_out)

    conv_c = short_conv_mixer(cv_c, sconv_w, sconv_w_out)
    conv_l = short_conv_mixer(cv_l, sconv_w, sconv_w_out)

    mix_c = gated_merge(g_c, (ssd_c, mla_c, pool_c, conv_c)) @ w_o
    mix_l = gated_merge(g_l, (ssd_l, mla_l, pool_l, conv_l)) @ w_o
    return mix_c, mix_l


def setup_inputs(seed: int = 0) -> dict:
    key = jax.random.key(seed)
    keys = iter(jax.random.split(key, 64))
    f32 = jnp.float32

    def normal(shape, scale):
        return jax.random.normal(next(keys), shape, f32) * scale

    def gain(shape):
        return 1.0 + 0.1 * jax.random.normal(next(keys), shape, f32)

    dt0 = jnp.exp(jax.random.uniform(next(keys), (DEPTH, 2, SSD_HEADS), f32,
                                     math.log(1e-3), math.log(1e-1)))
    dt_bias = dt0 + jnp.log(-jnp.expm1(-dt0))
    a_log = jnp.log(jax.random.uniform(next(keys), (DEPTH, 2, SSD_HEADS), f32, 1.0, 16.0))
    return {
        'x': normal((BATCH, SEQ, D_MODEL), 1.0),
        'c': normal((BATCH, D_MODEL), 1.0),
        'ctx': normal((BATCH, CTX_LEN, D_MODEL), 1.0),
        'c_ctx': normal((D_MODEL,), 1.0),
        'w_ada': normal((DEPTH, D_MODEL, N_MOD * D_MODEL), 0.5 * D_MODEL ** -0.5),
        'b_ada': normal((DEPTH, N_MOD * D_MODEL), 0.02),
        'ffn1_norm': gain((DEPTH, D_MODEL)),
        'ffn1_w_gate': normal((DEPTH, D_MODEL, D_FF), D_MODEL ** -0.5),
        'ffn1_w_up': normal((DEPTH, D_MODEL, D_FF), D_MODEL ** -0.5),
        'ffn1_w_down': normal((DEPTH, D_FF, D_MODEL), D_FF ** -0.5),
        'mix_norm': gain((DEPTH, D_MODEL)),
        'w_in': normal((DEPTH, D_MODEL, D_IN), D_MODEL ** -0.5),
        'ssd_conv_w': normal((DEPTH, SSD_CONV, SSD_XBC), SSD_CONV ** -0.5),
        'ssd_conv_b': normal((DEPTH, SSD_XBC), 0.02),
        'ssd_dt_bias': dt_bias,
        'ssd_a_log': a_log,
        'ssd_d': gain((DEPTH, SSD_HEADS)),
        'ssd_norm': gain((DEPTH, SSD_INNER)),
        'ssd_w_out': normal((DEPTH, SSD_INNER, D_MODEL), SSD_INNER ** -0.5),
        'mla_q_norm': gain((DEPTH, MLA_Q_RANK)),
        'mla_w_uq': normal((DEPTH, MLA_Q_RANK, MLA_HEADS * MLA_QK), MLA_Q_RANK ** -0.5),
        'mla_kv_norm': gain((DEPTH, MLA_KV_RANK)),
        'mla_w_uk': normal((DEPTH, MLA_KV_RANK, MLA_HEADS * MLA_NOPE), MLA_KV_RANK ** -0.5),
        'mla_w_uv': normal((DEPTH, MLA_KV_RANK, MLA_HEADS * MLA_V), MLA_KV_RANK ** -0.5),
        'mla_w_out': normal((DEPTH, MLA_HEADS * MLA_V, D_MODEL), (MLA_HEADS * MLA_V) ** -0.5),
        'pool_w': normal((DEPTH, len(POOL_WINDOWS), POOL_GROUP, POOL_GROUP), POOL_GROUP ** -0.5),
        'pool_scale': gain((DEPTH, POOL_WIDTH)),
        'pool_w_out': normal((DEPTH, POOL_WIDTH, D_MODEL), POOL_WIDTH ** -0.5),
        'sconv_w': normal((DEPTH, CONV_K, CONV_WIDTH), CONV_K ** -0.5),
        'sconv_w_out': normal((DEPTH, CONV_WIDTH, D_MODEL), CONV_WIDTH ** -0.5),
        'w_o': normal((DEPTH, D_MODEL, D_MODEL), D_MODEL ** -0.5),
        'ffn2_norm': gain((DEPTH, D_MODEL)),
        'ffn2_w_gate': normal((DEPTH, D_MODEL, D_FF), D_MODEL ** -0.5),
        'ffn2_w_up': normal((DEPTH, D_MODEL, D_FF), D_MODEL ** -0.5),
        'ffn2_w_down': normal((DEPTH, D_FF, D_MODEL), D_FF ** -0.5),
        'final_norm': gain((D_MODEL,)),
    }


def reference(x, c, ctx, c_ctx, w_ada, b_ada, ffn1_norm, ffn1_w_gate, ffn1_w_up, ffn1_w_down,
              mix_norm, w_in, ssd_conv_w, ssd_conv_b, ssd_dt_bias, ssd_a_log, ssd_d, ssd_norm,
              ssd_w_out, mla_q_norm, mla_w_uq, mla_kv_norm, mla_w_uk, mla_w_uv, mla_w_out,
              pool_w, pool_scale, pool_w_out, sconv_w, sconv_w_out, w_o, ffn2_norm, ffn2_w_gate,
              ffn2_w_up, ffn2_w_down, final_norm):
    cos, sin = axial_rope_tables(x.shape[1])
    h_l, h_c = x, ctx
    for i in range(DEPTH):
        m_l = jnp.split((jax.nn.silu(c) @ w_ada[i] + b_ada[i])[:, None, :], N_MOD, axis=-1)
        m_c = jnp.split(jax.nn.silu(c_ctx) @ w_ada[i] + b_ada[i], N_MOD, axis=-1)

        ffn1 = (ffn1_w_gate[i], ffn1_w_up[i], ffn1_w_down[i])
        h_c = h_c + FFN_RES * m_c[2] * swiglu(modulate(rms_norm(h_c, ffn1_norm[i]), m_c[0], m_c[1]), *ffn1)
        h_l = h_l + FFN_RES * m_l[2] * swiglu(modulate(rms_norm(h_l, ffn1_norm[i]), m_l[0], m_l[1]), *ffn1)

        a_c = modulate(rms_norm(h_c, mix_norm[i]), m_c[3], m_c[4])
        a_l = modulate(rms_norm(h_l, mix_norm[i]), m_l[3], m_l[4])
        o_c, o_l = token_mix(a_c, a_l, cos, sin, w_in[i], ssd_conv_w[i], ssd_conv_b[i], ssd_dt_bias[i],
                             ssd_a_log[i], ssd_d[i], ssd_norm[i], ssd_w_out[i], mla_q_norm[i],
                             mla_w_uq[i], mla_kv_norm[i], mla_w_uk[i], mla_w_uv[i], mla_w_out[i],
                             pool_w[i], pool_scale[i], pool_w_out[i], sconv_w[i], sconv_w_out[i], w_o[i])
        h_c = h_c + m_c[5] * o_c
        h_l = h_l + m_l[5] * o_l

        ffn2 = (ffn2_w_gate[i], ffn2_w_up[i], ffn2_w_down[i])
        h_c = h_c + FFN_RES * m_c[8] * swiglu(modulate(rms_norm(h_c, ffn2_norm[i]), m_c[6], m_c[7]), *ffn2)
        h_l = h_l + FFN_RES * m_l[8] * swiglu(modulate(rms_norm(h_l, ffn2_norm[i]), m_l[6], m_l[7]), *ffn2)
    return rms_norm(h_l, final_norm)
```

```python
import functools
import math

import jax
import jax.numpy as jnp
import numpy as np
from jax import lax
from jax.experimental import pallas as pl
from jax.experimental.pallas import tpu as pltpu

F32 = jnp.float32
BF16 = jnp.bfloat16

EPS = 1e-6
FFN_RES = 0.5
N_MOD = 9
GRID_W = 64
ROPE_THETA = 10000.0

SSD_HEADS = 16
SSD_HEAD_DIM = 64
SSD_GROUPS = 4
SSD_STATE = 128
SSD_CONV = 4
SSD_CHUNK = 128
SSD_INNER = SSD_HEADS * SSD_HEAD_DIM
SSD_XBC = SSD_INNER + 2 * SSD_GROUPS * SSD_STATE
HEADS_PER_GROUP = SSD_HEADS // SSD_GROUPS

MLA_HEADS = 8
MLA_Q_RANK = 384
MLA_KV_RANK = 256
MLA_NOPE = 128
MLA_ROPE = 64
MLA_V = 128
MLA_QK = MLA_NOPE + MLA_ROPE

POOL_WINDOWS = (2, 4, 8, 16)
POOL_GROUP = 256
POOL_WIDTH = POOL_GROUP * len(POOL_WINDOWS)
CONV_WIDTH = 1024
CONV_K = 3
N_BRANCH = 4

LANES = 128
BF16_ROWS = 16
HALO = BF16_ROWS
VMEM_CAP = 56 * 1024 * 1024
VMEM_SLACK = 12 * 1024 * 1024

ROW_TILE = 768
MIX_TILE = 256
FFN_CHUNK = 256


def _params(est_bytes, n_axes):
    return pltpu.CompilerParams(
        dimension_semantics=("arbitrary",) * n_axes,
        vmem_limit_bytes=int(min(est_bytes + VMEM_SLACK, VMEM_CAP)))


def _resident(shape):
    nd = len(shape)
    return pl.BlockSpec(shape, lambda *_: (0,) * nd, pipeline_mode=pl.Buffered(1))


def _nbytes(shape, dtype):
    return int(np.prod(shape)) * jnp.dtype(dtype).itemsize


def _dot(a, b):
    return jnp.dot(a, b, preferred_element_type=F32)


def _silu(x):
    return x * jax.nn.sigmoid(x)


def _rms(x, w):
    ms = jnp.mean(x * x, axis=-1, keepdims=True)
    return x * lax.rsqrt(ms + EPS) * w


def _row_mods(ml_ref, mc_ref, tile, tile_rows, n_ctx):
    rows = tile * tile_rows + lax.broadcasted_iota(jnp.int32, (tile_rows, 1), 0)
    is_ctx = rows < n_ctx

    def mod(k):
        return jnp.where(is_ctx, mc_ref[k:k + 1, :], ml_ref[0, k:k + 1, :])
    return mod


def _ada_kernel(c_ref, w_ref, b_ref, o_ref):
    s = _silu(c_ref[...]).astype(BF16)
    o_ref[0] = _dot(s, w_ref[0].astype(BF16)) + b_ref[0]


def _adaln(cc, w_ada, b_ada):
    depth, d, n = w_ada.shape
    rows = cc.shape[0]
    tn = n // 8
    est = 2 * (_nbytes((d, tn), F32) + _nbytes((rows, tn), F32)) + _nbytes((d, tn), BF16)
    return pl.pallas_call(
        _ada_kernel,
        grid=(depth, n // tn),
        in_specs=[pl.BlockSpec((rows, d), lambda i, j: (0, 0)),
                  pl.BlockSpec((1, d, tn), lambda i, j: (i, 0, j)),
                  pl.BlockSpec((1, 1, tn), lambda i, j: (i, 0, j))],
        out_specs=pl.BlockSpec((1, rows, tn), lambda i, j: (i, 0, j)),
        out_shape=jax.ShapeDtypeStruct((depth, rows, n), F32),
        compiler_params=_params(est, 2),
        name="adaln",
    )(cc, w_ada, b_ada.reshape(depth, 1, n))


def _ffn_kernel(h_ref, ml_ref, mc_ref, nw_ref, wg_ref, wu_ref, wd_ref, o_ref, p_scr, *, n_ctx, mod0):
    tm = h_ref.shape[1]
    d_ff = wg_ref.shape[1]
    mod = _row_mods(ml_ref, mc_ref, pl.program_id(1), tm, n_ctx)
    h = h_ref[0]
    a = (_rms(h, nw_ref[...]) * (1.0 + mod(mod0 + 1)) + mod(mod0)).astype(BF16)
    for j in range(d_ff // FFN_CHUNK):
        cols = slice(j * FFN_CHUNK, (j + 1) * FFN_CHUNK)
        g = _dot(a, wg_ref[:, cols])
        u = _dot(a, wu_ref[:, cols])
        p_scr[:, cols] = (_silu(g) * u).astype(BF16)
    y = _dot(p_scr[...], wd_ref[...])
    o_ref[0] = h + (FFN_RES * mod(mod0 + 2)) * y


def _ffn(h, ml, mc, norm_w, wg, wu, wd, *, n_ctx, mod0):
    b, s, d = h.shape
    d_ff = wg.shape[1]
    tm = ROW_TILE
    assert s % tm == 0 and d_ff % FFN_CHUNK == 0
    est = (4 * _nbytes((tm, d), F32) + 3 * _nbytes((d, d_ff), BF16) + _nbytes((tm, d_ff), BF16)
           + 4 * _nbytes((tm, FFN_CHUNK), F32) + 2 * _nbytes((tm, d), F32))
    return pl.pallas_call(
        functools.partial(_ffn_kernel, n_ctx=n_ctx, mod0=mod0),
        grid=(b, s // tm),
        in_specs=[pl.BlockSpec((1, tm, d), lambda i, t: (i, t, 0)),
                  pl.BlockSpec((1, N_MOD, d), lambda i, t: (i, 0, 0)),
                  _resident(mc.shape), _resident((1, d)),
                  _resident(wg.shape), _resident(wu.shape), _resident(wd.shape)],
        out_specs=pl.BlockSpec((1, tm, d), lambda i, t: (i, t, 0)),
        out_shape=jax.ShapeDtypeStruct(h.shape, F32),
        scratch_shapes=[pltpu.VMEM((tm, d_ff), BF16)],
        compiler_params=_params(est, 2),
        name="ffn",
    )(h, ml, mc, norm_w.reshape(1, d), wg, wu, wd)


def _proj_seq_kernel(h_ref, ml_ref, mc_ref, nw_ref, cos_ref, sin_ref, wz, wxbc, wdt, wqd, wkvd, wkr,
                     qnw_ref, kvnw_ref, wuqn, wuqp, wuqr, wuk, wuv,
                     z_o, xbc_o, dt_o, qn_o, qp_o, kn_o, kp_o, v_o, *, n_ctx, qk_scale):
    tm = h_ref.shape[1]
    mod = _row_mods(ml_ref, mc_ref, pl.program_id(1), tm, n_ctx)
    a = (_rms(h_ref[0], nw_ref[...]) * (1.0 + mod(4)) + mod(3)).astype(BF16)
    z_o[0] = _dot(a, wz[...]).astype(BF16)
    xbc_o[0] = _dot(a, wxbc[...]).astype(BF16)
    dt_o[0] = _dot(a, wdt[...])
    cos = cos_ref[...]
    sin = sin_ref[...]
    cos_h = jnp.concatenate([cos] * MLA_HEADS, axis=1)
    sin_h = jnp.concatenate([sin] * MLA_HEADS, axis=1)
    qn = _rms(_dot(a, wqd[...]), qnw_ref[...]).astype(BF16)
    qn_o[0] = (_dot(qn, wuqn[...]) * qk_scale).astype(BF16)
    qp_o[0] = ((_dot(qn, wuqp[...]) * cos_h + _dot(qn, wuqr[...]) * sin_h) * qk_scale).astype(BF16)
    ckv = _rms(_dot(a, wkvd[...]), kvnw_ref[...]).astype(BF16)
    kn_o[0] = _dot(ckv, wuk[...]).astype(BF16)
    v_o[0] = _dot(ckv, wuv[...]).astype(BF16)
    kk = _dot(a, wkr[...])
    kp_o[0] = (kk[:, :LANES] * cos + kk[:, LANES:] * sin).astype(BF16)


def _proj_seq(h, ml, mc, norm_w, cos_t, sin_t, w, *, n_ctx):
    b, s, d = h.shape
    tm = ROW_TILE
    assert s % tm == 0
    weights = [w["wz"], w["wxbc"], w["wdt"], w["wqd"], w["wkvd"], w["wkr"], w["q_norm"], w["kv_norm"],
               w["wuqn"], w["wuqp"], w["wuqr"], w["wuk"], w["wuv"]]
    widths = [(SSD_INNER, BF16), (SSD_XBC, BF16), (LANES, F32), (MLA_HEADS * MLA_NOPE, BF16),
              (MLA_HEADS * LANES, BF16), (MLA_HEADS * MLA_NOPE, BF16), (LANES, BF16),
              (MLA_HEADS * MLA_V, BF16)]
    tile = lambda i, t: (i, t, 0)
    est = (2 * _nbytes((tm, d), F32) + sum(_nbytes(x.shape, x.dtype) for x in weights)
           + 2 * sum(_nbytes((tm, n), dt) for n, dt in widths) + 4 * _nbytes((tm, SSD_XBC), F32))
    return pl.pallas_call(
        functools.partial(_proj_seq_kernel, n_ctx=n_ctx, qk_scale=MLA_QK ** -0.5),
        grid=(b, s // tm),
        in_specs=[pl.BlockSpec((1, tm, d), tile),
                  pl.BlockSpec((1, N_MOD, d), lambda i, t: (i, 0, 0)),
                  _resident(mc.shape), _resident((1, d)),
                  pl.BlockSpec((tm, LANES), lambda i, t: (t, 0)),
                  pl.BlockSpec((tm, LANES), lambda i, t: (t, 0))]
                 + [_resident(x.shape) for x in weights],
        out_specs=[pl.BlockSpec((1, tm, n), tile) for n, _ in widths],
        out_shape=[jax.ShapeDtypeStruct((b, s, n), dt) for n, dt in widths],
        compiler_params=_params(est, 2),
        name="proj_seq",
    )(h, ml, mc, norm_w.reshape(1, d), cos_t, sin_t, *weights)


def _proj_loc_kernel(h_ref, ml_ref, mc_ref, nw_ref, wpv, wcv, wg, pv_o, cv_o, g_o, *, n_ctx):
    tm = h_ref.shape[1]
    mod = _row_mods(ml_ref, mc_ref, pl.program_id(1), tm, n_ctx)
    a = (_rms(h_ref[0], nw_ref[...]) * (1.0 + mod(4)) + mod(3)).astype(BF16)
    pv_o[0] = _dot(a, wpv[...]).astype(BF16)
    cv_o[0] = _dot(a, wcv[...]).astype(BF16)
    g_o[0] = _dot(a, wg[...]).astype(BF16)


def _proj_loc(h, ml, mc, norm_w, w, *, n_ctx):
    b, s, d = h.shape
    tm = ROW_TILE
    weights = [w["wpv"], w["wcv"], w["wg"]]
    widths = [x.shape[1] for x in weights]
    tile = lambda i, t: (i, t, 0)
    est = (2 * _nbytes((tm, d), F32) + sum(_nbytes(x.shape, BF16) for x in weights)
           + 2 * sum(_nbytes((tm, n), BF16) for n in widths) + 2 * _nbytes((tm, max(widths)), F32))
    return pl.pallas_call(
        functools.partial(_proj_loc_kernel, n_ctx=n_ctx),
        grid=(b, s // tm),
        in_specs=[pl.BlockSpec((1, tm, d), tile),
                  pl.BlockSpec((1, N_MOD, d), lambda i, t: (i, 0, 0)),
                  _resident(mc.shape), _resident((1, d))]
                 + [_resident(x.shape) for x in weights],
        out_specs=[pl.BlockSpec((1, tm, n), tile) for n in widths],
        out_shape=[jax.ShapeDtypeStruct((b, s, n), BF16) for n in widths],
        compiler_params=_params(est, 2),
        name="proj_loc",
    )(h, ml, mc, norm_w.reshape(1, d), *weights)


def _split_bf16(x, parts):
    out = []
    for _ in range(parts):
        p = x.astype(BF16)
        out.append(p)
        x = x - p.astype(F32)
    return out


def _ssd_kernel(xbc_ref, dt_ref, cw_ref, cb_ref, bias_ref, alog_ref, dskip_ref, expand_ref, tri_ref,
                y_ref, xs_s, c_s, bt_s, ext_s, st_s, *, n_ctx):
    q = SSD_CHUNK
    n = SSD_STATE
    s_len = xbc_ref.shape[1]
    n_chunks = s_len // q
    ctx_chunks = n_ctx // q
    direction = pl.program_id(1)
    gw = HEADS_PER_GROUP * SSD_HEAD_DIM
    gn = SSD_GROUPS * n

    @pl.when(direction == 0)
    def _conv_phase():
        def chunk(c, carry):
            r0 = pl.multiple_of(c * q, q)
            p0 = pl.multiple_of(jnp.maximum(r0 - HALO, 0), HALO)
            n0 = pl.multiple_of(jnp.minimum(r0 + q, s_len - HALO), HALO)
            seq_start = jnp.logical_or(c == 0, c == ctx_chunks)
            seq_end = jnp.logical_or(c == ctx_chunks - 1, c == n_chunks - 1)
            prev = xbc_ref[0, pl.ds(p0, HALO), :].astype(F32)
            nxt = xbc_ref[0, pl.ds(n0, HALO), :].astype(F32)
            ext_s[0:HALO, :] = jnp.where(seq_start, 0.0, prev)
            ext_s[HALO:HALO + q, :] = xbc_ref[0, pl.ds(r0, q), :].astype(F32)
            ext_s[HALO + q:, :] = jnp.where(seq_end, 0.0, nxt)
            acc = jnp.broadcast_to(cb_ref[...], (q, SSD_XBC))
            pad_l = SSD_CONV // 2
            for k in range(SSD_CONV):
                off = HALO - pad_l + k
                acc = acc + cw_ref[k:k + 1, :] * ext_s[off:off + q, :]
            act = _silu(acc)
            xs_s[pl.ds(r0, q), :] = act[:, :SSD_INNER].astype(BF16)
            c_s[pl.ds(r0, q), :] = act[:, SSD_INNER + gn:].astype(BF16)
            for g in range(SSD_GROUPS):
                b_g = act[:, SSD_INNER + g * n:SSD_INNER + (g + 1) * n]
                b0 = pl.multiple_of(c * gn + g * n, n)
                bt_s[pl.ds(b0, n), :] = b_g.T.astype(BF16)
            return carry
        lax.fori_loop(0, n_chunks, chunk, 0)

    def scan(d):
        st_s[...] = jnp.zeros(st_s.shape, F32)
        li = lax.broadcasted_iota(jnp.int32, (q, q), 0)
        si = lax.broadcasted_iota(jnp.int32, (q, q), 1)
        causal = (li >= si) if d == 0 else (li <= si)
        head_of_lane = lax.broadcasted_iota(jnp.int32, (1, gw), 1) // SSD_HEAD_DIM
        last = q - 1 if d == 0 else 0
        a_row = -jnp.exp(alog_ref[...])
        tri = tri_ref[d]
        expand = expand_ref[d]

        def chunk(i, carry):
            if d == 0:
                c = i
            else:
                c = jnp.where(i < ctx_chunks, ctx_chunks - 1 - i, n_chunks - 1 - i + ctx_chunks)
            r0 = pl.multiple_of(c * q, q)
            x = dt_ref[0, pl.ds(r0, q), :] + bias_ref[...]
            dtv = jnp.maximum(x, 0.0) + jnp.log1p(jnp.exp(-jnp.abs(x)))
            da = dtv * a_row
            cs = sum(_dot(tri, p) for p in _split_bf16(da, 3))
            tot = cs[last:last + 1, :]
            cs_t = cs.T
            dt_t = dtv.T
            stack = jnp.concatenate(
                [dtv * jnp.exp(tot - cs), jnp.exp(cs), jnp.broadcast_to(jnp.exp(tot), (8, LANES))], axis=0)
            ex = sum(_dot(p, expand) for p in _split_bf16(stack, 2))
            w_exp = ex[:q]
            ecs_exp = ex[q:2 * q]
            etot_exp = ex[2 * q:2 * q + 1]
            xs = xs_s[pl.ds(r0, q), :]
            xs_f = xs.astype(F32)
            xe = (xs_f * w_exp).astype(BF16)
            cm = c_s[pl.ds(r0, q), :]
            for g in range(SSD_GROUPS):
                gc = slice(g * gw, (g + 1) * gw)
                c_g = cm[:, g * n:(g + 1) * n]
                b0 = pl.multiple_of(c * gn + g * n, n)
                bt_g = bt_s[pl.ds(b0, n), :]
                cb = _dot(c_g, bt_g)
                st_g = st_s[:, gc]
                y_off = _dot(c_g, st_g.astype(BF16))
                xs_g = xs_f[:, gc]
                m_parts, x_parts = [], []
                for r in range(HEADS_PER_GROUP):
                    col = d * SSD_HEADS + g * HEADS_PER_GROUP + r
                    seg = cs[:, col:col + 1] - cs_t[col:col + 1, :]
                    decay = jnp.exp(jnp.where(causal, seg, -jnp.inf))
                    m_parts.append((cb * decay * dt_t[col:col + 1, :]).astype(BF16))
                    x_parts.append(jnp.where(head_of_lane == r, xs_g, 0.0).astype(BF16))
                y_g = _dot(jnp.concatenate(m_parts, axis=1), jnp.concatenate(x_parts, axis=0))
                y_g = y_g + y_off * ecs_exp[:, gc]
                if d == 0:
                    y_g = y_g + xs_g * dskip_ref[:, gc]
                y_ref[0, 0, pl.ds(r0, q), gc] = y_g.astype(BF16)
                st_s[:, gc] = st_g * etot_exp[:, gc] + _dot(bt_g, xe[:, gc])
            return carry
        lax.fori_loop(0, n_chunks, chunk, 0)

    @pl.when(direction == 0)
    def _forward():
        scan(0)

    @pl.when(direction == 1)
    def _backward():
        scan(1)


def _ssd(xbc, dt, conv_w, conv_b, dt_bias, a_log, d_skip, *, n_ctx):
    b, s, _ = xbc.shape
    q = SSD_CHUNK
    assert s % q == 0 and n_ctx % q == 0
    pad = LANES - 2 * SSD_HEADS
    bias_row = jnp.pad(dt_bias.reshape(1, -1), ((0, 0), (0, pad)))
    alog_row = jnp.pad(a_log.reshape(1, -1), ((0, 0), (0, pad)))
    dskip_row = jnp.repeat(d_skip, SSD_HEAD_DIM).reshape(1, SSD_INNER)
    e = np.zeros((2, LANES, SSD_INNER), np.float32)
    for d in range(2):
        for h in range(SSD_HEADS):
            e[d, d * SSD_HEADS + h, h * SSD_HEAD_DIM:(h + 1) * SSD_HEAD_DIM] = 1.0
    low = np.tril(np.ones((q, q), np.float32))
    tri = np.stack([low, low.T])
    est = (2 * _nbytes((s, SSD_XBC), BF16) + 2 * _nbytes((s, LANES), F32) + 2 * _nbytes((s, SSD_INNER), BF16)
           + _nbytes((s, SSD_INNER), BF16) + 2 * _nbytes((s, SSD_GROUPS * SSD_STATE), BF16)
           + _nbytes((q + 2 * HALO, SSD_XBC), F32) + _nbytes((SSD_STATE, SSD_INNER), F32)
           + 6 * _nbytes((q, SSD_XBC), F32))
    return pl.pallas_call(
        functools.partial(_ssd_kernel, n_ctx=n_ctx),
        grid=(b, 2),
        in_specs=[pl.BlockSpec((1, s, SSD_XBC), lambda i, d: (i, 0, 0)),
                  pl.BlockSpec((1, s, LANES), lambda i, d: (i, 0, 0)),
                  _resident(conv_w.shape), _resident((1, SSD_XBC)), _resident((1, LANES)),
                  _resident((1, LANES)), _resident((1, SSD_INNER)),
                  _resident(e.shape), _resident(tri.shape)],
        out_specs=pl.BlockSpec((1, 1, s, SSD_INNER), lambda i, d: (i, d, 0, 0)),
        out_shape=jax.ShapeDtypeStruct((b, 2, s, SSD_INNER), BF16),
        scratch_shapes=[pltpu.VMEM((s, SSD_INNER), BF16),
                        pltpu.VMEM((s, SSD_GROUPS * SSD_STATE), BF16),
                        pltpu.VMEM((s // q * SSD_GROUPS * SSD_STATE, q), BF16),
                        pltpu.VMEM((q + 2 * HALO, SSD_XBC), F32),
                        pltpu.VMEM((SSD_STATE, SSD_INNER), F32)],
        compiler_params=_params(est, 2),
        name="ssd",
    )(xbc, dt, conv_w, conv_b.reshape(1, -1), bias_row, alog_row, dskip_row,
      jnp.asarray(e, BF16), jnp.asarray(tri, BF16))


def _attn_kernel(qn_ref, qp_ref, kn_ref, kp_ref, v_ref, o_ref, *, n_ctx):
    tq = qn_ref.shape[1]
    s_len = kn_ref.shape[1]

    def attend(n_keys):
        kp = kp_ref[0, :n_keys, :]
        for h in range(MLA_HEADS):
            hc = slice(h * LANES, (h + 1) * LANES)
            qc = jnp.concatenate([qn_ref[0, :, hc], qp_ref[0, :, hc]], axis=1)
            kc = jnp.concatenate([kn_ref[0, :n_keys, hc], kp], axis=1)
            s = lax.dot_general(qc, kc, (((1,), (1,)), ((), ())), preferred_element_type=F32)
            p = jnp.exp(s - jnp.max(s, axis=-1, keepdims=True))
            denom = jnp.sum(p, axis=-1, keepdims=True)
            o = _dot(p.astype(BF16), v_ref[0, :n_keys, hc])
            o_ref[0, :, hc] = (o / denom).astype(BF16)

    is_ctx_tile = pl.program_id(1) < n_ctx // tq

    @pl.when(is_ctx_tile)
    def _ctx():
        attend(n_ctx)

    @pl.when(jnp.logical_not(is_ctx_tile))
    def _lat():
        attend(s_len)


def _attention(qn, qp, kn, kp, v, *, n_ctx):
    b, s, w = qn.shape
    tq = MIX_TILE
    assert s % tq == 0 and n_ctx % tq == 0
    tile = lambda i, t: (i, t, 0)
    whole = lambda i, t: (i, 0, 0)
    est = (4 * _nbytes((tq, w), BF16) + 2 * (2 * _nbytes((s, w), BF16) + _nbytes((s, LANES), BF16))
           + 2 * _nbytes((tq, w), BF16) + 4 * _nbytes((tq, s), F32) + _nbytes((s, 2 * LANES), BF16))
    return pl.pallas_call(
        functools.partial(_attn_kernel, n_ctx=n_ctx),
        grid=(b, s // tq),
        in_specs=[pl.BlockSpec((1, tq, w), tile), pl.BlockSpec((1, tq, w), tile),
                  pl.BlockSpec((1, s, w), whole), pl.BlockSpec((1, s, LANES), whole),
                  pl.BlockSpec((1, s, w), whole)],
        out_specs=pl.BlockSpec((1, tq, w), tile),
        out_shape=jax.ShapeDtypeStruct((b, s, w), BF16),
        compiler_params=_params(est, 2),
        name="attention",
    )(qn, qp, kn, kp, v)


def _merge_kernel(h_ref, ml_ref, mc_ref, y_ref, z_ref, att_ref,
                  pv_ref, pvp_ref, pvn_ref, cv_ref, cvp_ref, cvn_ref, g_ref,
                  snw_ref, wso, wmo, pw_ref, ps_ref, wpo, cw_ref, wco, wo,
                  o_ref, ext_p, ext_u, *, n_ctx):
    tm = h_ref.shape[1]
    d = h_ref.shape[2]
    t = pl.program_id(1)
    s_len = pl.num_programs(1) * tm
    row0 = t * tm
    mod = _row_mods(ml_ref, mc_ref, t, tm, n_ctx)
    seq_start = jnp.logical_or(row0 == 0, row0 == n_ctx)
    seq_end = jnp.logical_or(row0 + tm == n_ctx, row0 + tm == s_len)
    body = slice(HALO, HALO + tm)

    y = y_ref[0, 0].astype(F32) + y_ref[0, 1].astype(F32)
    ssd_b = _dot(_rms(y * _silu(z_ref[0].astype(F32)), snw_ref[...]).astype(BF16), wso[...])

    mla_b = _dot(att_ref[0], wmo[...])

    ext_p[0:HALO, :] = jnp.where(seq_start, 0.0, pvp_ref[0].astype(F32))
    ext_p[body, :] = pv_ref[0].astype(F32)
    ext_p[HALO + tm:, :] = jnp.where(seq_end, 0.0, pvn_ref[0].astype(F32))
    rows = row0 + lax.broadcasted_iota(jnp.int32, (tm, 1), 0)
    in_ctx = rows < n_ctx
    pos = jnp.where(in_ctx, rows, rows - n_ctx)
    seq_n = jnp.where(in_ctx, n_ctx, s_len - n_ctx)
    pooled = []
    for gi, win in enumerate(POOL_WINDOWS):
        gc = slice(gi * POOL_GROUP, (gi + 1) * POOL_GROUP)
        left = win // 2
        acc = ext_p[HALO - left:HALO - left + tm, gc]
        for j in range(1, win):
            acc = acc + ext_p[HALO - left + j:HALO - left + j + tm, gc]
        lo = jnp.clip(pos - left, 0, seq_n)
        hi = jnp.clip(pos - left + win, 0, seq_n)
        p_g = acc / (hi - lo).astype(F32) - ext_p[body, gc]
        pooled.append(_dot(p_g.astype(BF16), pw_ref[gi]))
    pool_y = jnp.concatenate(pooled, axis=1) * ps_ref[...]
    pool_b = _dot(pool_y.astype(BF16), wpo[...])

    def gated_in(ref):
        return ref[0, :, CONV_WIDTH:2 * CONV_WIDTH].astype(F32) * ref[0, :, 2 * CONV_WIDTH:].astype(F32)
    ext_u[0:HALO, :] = jnp.where(seq_start, 0.0, gated_in(cvp_ref))
    ext_u[body, :] = gated_in(cv_ref)
    ext_u[HALO + tm:, :] = jnp.where(seq_end, 0.0, gated_in(cvn_ref))
    conv = cw_ref[0:1, :] * ext_u[HALO - 1:HALO - 1 + tm, :]
    for k in range(1, CONV_K):
        conv = conv + cw_ref[k:k + 1, :] * ext_u[HALO - 1 + k:HALO - 1 + k + tm, :]
    conv_b = _dot((cv_ref[0, :, :CONV_WIDTH].astype(F32) * conv).astype(BF16), wco[...])

    merged = jax.nn.sigmoid(g_ref[0, :, 0:d].astype(F32)) * ssd_b
    for k, br in enumerate((mla_b, pool_b, conv_b), start=1):
        merged = merged + jax.nn.sigmoid(g_ref[0, :, k * d:(k + 1) * d].astype(F32)) * br
    o_ref[0] = h_ref[0] + mod(5) * _dot(merged.astype(BF16), wo[...])


def _merge(h, ml, mc, y2, z, att, pv, cv, g, w, *, n_ctx):
    b, s, d = h.shape
    tm = MIX_TILE
    assert s % tm == 0 and n_ctx % tm == 0 and tm % HALO == 0
    hb = tm // HALO
    last_hb = s // HALO - 1
    tile = lambda i, t: (i, t, 0)
    prev = lambda i, t: (i, jnp.maximum(t * hb - 1, 0), 0)
    nxt = lambda i, t: (i, jnp.minimum((t + 1) * hb, last_hb), 0)
    weights = [w["ssd_norm"], w["ssd_w_out"], w["mla_w_out"], w["pool_w"], w["pool_scale"], w["pool_w_out"],
               w["sconv_w"], w["sconv_w_out"], w["w_o"]]
    stream = (2 * _nbytes((tm, d), F32) + _nbytes((tm, 4 * SSD_INNER + POOL_WIDTH + 3 * CONV_WIDTH + N_BRANCH * d), BF16)
              + 2 * _nbytes((HALO, POOL_WIDTH + 3 * CONV_WIDTH), BF16))
    est = (2 * stream + sum(_nbytes(x.shape, x.dtype) for x in weights)
           + 2 * _nbytes((tm + 2 * HALO, d), F32) + 10 * _nbytes((tm, d), F32))
    return pl.pallas_call(
        functools.partial(_merge_kernel, n_ctx=n_ctx),
        grid=(b, s // tm),
        in_specs=[pl.BlockSpec((1, tm, d), tile),
                  pl.BlockSpec((1, N_MOD, d), lambda i, t: (i, 0, 0)),
                  _resident(mc.shape),
                  pl.BlockSpec((1, 2, tm, SSD_INNER), lambda i, t: (i, 0, t, 0)),
                  pl.BlockSpec((1, tm, SSD_INNER), tile),
                  pl.BlockSpec((1, tm, MLA_HEADS * MLA_V), tile),
                  pl.BlockSpec((1, tm, POOL_WIDTH), tile),
                  pl.BlockSpec((1, HALO, POOL_WIDTH), prev),
                  pl.BlockSpec((1, HALO, POOL_WIDTH), nxt),
                  pl.BlockSpec((1, tm, 3 * CONV_WIDTH), tile),
                  pl.BlockSpec((1, HALO, 3 * CONV_WIDTH), prev),
                  pl.BlockSpec((1, HALO, 3 * CONV_WIDTH), nxt),
                  pl.BlockSpec((1, tm, N_BRANCH * d), tile)]
                 + [_resident(x.shape) for x in weights],
        out_specs=pl.BlockSpec((1, tm, d), tile),
        out_shape=jax.ShapeDtypeStruct(h.shape, F32),
        scratch_shapes=[pltpu.VMEM((tm + 2 * HALO, POOL_WIDTH), F32),
                        pltpu.VMEM((tm + 2 * HALO, CONV_WIDTH), F32)],
        compiler_params=_params(est, 2),
        name="merge",
    )(h, ml, mc, y2, z, att, pv, pv, pv, cv, cv, cv, g, *weights)


def _final_kernel(h_ref, w_ref, o_ref):
    o_ref[0] = _rms(h_ref[0], w_ref[...])


def _final_norm(h, w, *, n_ctx):
    b, s, d = h.shape
    tm = MIX_TILE
    assert n_ctx % tm == 0
    off = n_ctx // tm
    return pl.pallas_call(
        _final_kernel,
        grid=(b, (s - n_ctx) // tm),
        in_specs=[pl.BlockSpec((1, tm, d), lambda i, t: (i, t + off, 0)), _resident((1, d))],
        out_specs=pl.BlockSpec((1, tm, d), lambda i, t: (i, t, 0)),
        out_shape=jax.ShapeDtypeStruct((b, s - n_ctx, d), F32),
        compiler_params=_params(4 * _nbytes((tm, d), F32), 2),
        name="final_norm",
    )(h, w.reshape(1, d))


def _rope_tables(n_ctx, n_lat):
    rows = n_lat // GRID_W
    row = np.repeat(np.arange(rows), GRID_W).astype(np.float32)
    col = np.tile(np.arange(GRID_W), rows).astype(np.float32)
    half = MLA_ROPE // 2
    inv = (1.0 / (ROPE_THETA ** (jnp.arange(0, half, 2, dtype=F32) / half)))
    ar = jnp.asarray(row)[:, None] * inv
    ac = jnp.asarray(col)[:, None] * inv
    ang = jnp.concatenate([ar, ar, ac, ac], axis=-1)
    pad = LANES - MLA_ROPE
    cos = jnp.concatenate([jnp.ones((n_ctx, MLA_ROPE), F32), jnp.cos(ang)], axis=0)
    sin = jnp.concatenate([jnp.zeros((n_ctx, MLA_ROPE), F32), jnp.sin(ang)], axis=0)
    return jnp.pad(cos, ((0, 0), (0, pad))), jnp.pad(sin, ((0, 0), (0, pad)))


def _rot_cols(w):
    w1, w2, w3, w4 = jnp.split(w, 4, axis=-1)
    return jnp.concatenate([-w2, w1, -w4, w3], axis=-1)


def _layer_weights(i, p):
    w_in = p["w_in"][i]
    d = w_in.shape[0]
    splits = np.cumsum([SSD_INNER, SSD_XBC, 2 * SSD_HEADS, MLA_Q_RANK, MLA_KV_RANK, MLA_ROPE,
                        POOL_WIDTH, 3 * CONV_WIDTH])
    wz, wxbc, wdt, wqd, wkvd, wkr, wpv, wcv, wg = jnp.split(w_in, splits, axis=1)
    lane_pad = lambda x: jnp.pad(x, ((0, 0), (0, LANES - x.shape[1])))
    uq = p["mla_w_uq"][i].reshape(MLA_Q_RANK, MLA_HEADS, MLA_QK)
    uq_pe = uq[:, :, MLA_NOPE:]
    head_pad = lambda x: jnp.pad(x, ((0, 0), (0, 0), (0, LANES - MLA_ROPE))).reshape(MLA_Q_RANK, MLA_HEADS * LANES)
    bf = lambda x: x.astype(BF16)
    return {
        "wz": bf(wz), "wxbc": bf(wxbc), "wdt": bf(lane_pad(wdt)), "wqd": bf(wqd), "wkvd": bf(wkvd),
        "wkr": bf(jnp.concatenate([lane_pad(wkr), lane_pad(_rot_cols(wkr))], axis=1)),
        "q_norm": p["mla_q_norm"][i].reshape(1, -1), "kv_norm": p["mla_kv_norm"][i].reshape(1, -1),
        "wuqn": bf(uq[:, :, :MLA_NOPE].reshape(MLA_Q_RANK, MLA_HEADS * MLA_NOPE)),
        "wuqp": bf(head_pad(uq_pe)), "wuqr": bf(head_pad(_rot_cols(uq_pe))),
        "wuk": bf(p["mla_w_uk"][i]), "wuv": bf(p["mla_w_uv"][i]),
        "wpv": bf(wpv), "wcv": bf(wcv), "wg": bf(wg),
        "ssd_norm": p["ssd_norm"][i].reshape(1, -1), "ssd_w_out": bf(p["ssd_w_out"][i]),
        "mla_w_out": bf(p["mla_w_out"][i]), "pool_w": bf(p["pool_w"][i]),
        "pool_scale": p["pool_scale"][i].reshape(1, -1), "pool_w_out": bf(p["pool_w_out"][i]),
        "sconv_w": p["sconv_w"][i], "sconv_w_out": bf(p["sconv_w_out"][i]), "w_o": bf(p["w_o"][i]),
    }


def kernel(x, c, ctx, c_ctx, w_ada, b_ada, ffn1_norm, ffn1_w_gate, ffn1_w_up, ffn1_w_down, mix_norm, w_in, ssd_conv_w, ssd_conv_b, ssd_dt_bias, ssd_a_log, ssd_d, ssd_norm, ssd_w_out, mla_q_norm, mla_w_uq, mla_kv_norm, mla_w_uk, mla_w_uv, mla_w_out, pool_w, pool_scale, pool_w_out, sconv_w, sconv_w_out, w_o, ffn2_norm, ffn2_w_gate, ffn2_w_up, ffn2_w_down, final_norm):
    p = dict(w_in=w_in, ssd_norm=ssd_norm, ssd_w_out=ssd_w_out, mla_q_norm=mla_q_norm, mla_w_uq=mla_w_uq,
             mla_kv_norm=mla_kv_norm, mla_w_uk=mla_w_uk, mla_w_uv=mla_w_uv, mla_w_out=mla_w_out,
             pool_w=pool_w, pool_scale=pool_scale, pool_w_out=pool_w_out, sconv_w=sconv_w,
             sconv_w_out=sconv_w_out, w_o=w_o)
    bsz, n_lat, d = x.shape
    n_ctx = ctx.shape[1]
    depth = w_ada.shape[0]

    mod_rows = -(-(bsz + 1) // 8) * 8
    cc = jnp.concatenate([c, c_ctx[None, :], jnp.zeros((mod_rows - bsz - 1, d), F32)], axis=0)
    mods = _adaln(cc, w_ada, b_ada).reshape(depth, mod_rows, N_MOD, d)

    cos_t, sin_t = _rope_tables(n_ctx, n_lat)
    h = jnp.concatenate([ctx, x], axis=1)
    for i in range(depth):
        ml, mc = mods[i, :bsz], mods[i, bsz]
        w = _layer_weights(i, p)
        h = _ffn(h, ml, mc, ffn1_norm[i], ffn1_w_gate[i].astype(BF16), ffn1_w_up[i].astype(BF16),
                 ffn1_w_down[i].astype(BF16), n_ctx=n_ctx, mod0=0)
        z, xbc, dt, qn, qp, kn, kp, v = _proj_seq(h, ml, mc, mix_norm[i], cos_t, sin_t, w, n_ctx=n_ctx)
        pv, cv, g = _proj_loc(h, ml, mc, mix_norm[i], w, n_ctx=n_ctx)
        y2 = _ssd(xbc, dt, ssd_conv_w[i], ssd_conv_b[i], ssd_dt_bias[i], ssd_a_log[i], ssd_d[i], n_ctx=n_ctx)
        att = _attention(qn, qp, kn, kp, v, n_ctx=n_ctx)
        h = _merge(h, ml, mc, y2, z, att, pv, cv, g, w, n_ctx=n_ctx)
        h = _ffn(h, ml, mc, ffn2_norm[i], ffn2_w_gate[i].astype(BF16), ffn2_w_up[i].astype(BF16),
                 ffn2_w_down[i].astype(BF16), n_ctx=n_ctx, mod0=6)
    return _final_norm(h, final_norm, n_ctx=n_ctx)
```

```python
import functools
import math

import jax
import jax.numpy as jnp
import numpy as np
from jax import lax
from jax.experimental import pallas as pl
from jax.experimental.pallas import tpu as pltpu

F32 = jnp.float32
BF16 = jnp.bfloat16

EPS = 1e-6
LOG2E = math.log2(math.e)
FFN_RES = 0.5
N_MOD = 9
GRID_W = 64
ROPE_THETA = 10000.0

SSD_HEADS = 16
SSD_HEAD_DIM = 64
SSD_GROUPS = 4
SSD_STATE = 128
SSD_CONV = 4
SSD_CHUNK = 128
SSD_INNER = SSD_HEADS * SSD_HEAD_DIM
SSD_BC = SSD_GROUPS * SSD_STATE
SSD_XBC = SSD_INNER + 2 * SSD_BC
HEADS_PER_GROUP = SSD_HEADS // SSD_GROUPS

MLA_HEADS = 8
MLA_Q_RANK = 384
MLA_KV_RANK = 256
MLA_NOPE = 128
MLA_ROPE = 64
MLA_V = 128
MLA_QK = MLA_NOPE + MLA_ROPE

POOL_WINDOWS = (2, 4, 8, 16)
POOL_GROUP = 256
POOL_WIDTH = POOL_GROUP * len(POOL_WINDOWS)
CONV_WIDTH = 1024
CONV_K = 3
N_BRANCH = 4

LANES = 128
BF16_ROWS = 16
HALO = BF16_ROWS
VMEM_CAP = 56 * 1024 * 1024
VMEM_SLACK = 12 * 1024 * 1024

ROW_TILE = 768
MIX_TILE = 256
FFN_CHUNK = 256


def _params(est_bytes, n_axes):
    return pltpu.CompilerParams(
        dimension_semantics=("arbitrary",) * n_axes,
        vmem_limit_bytes=int(min(est_bytes + VMEM_SLACK, VMEM_CAP)))


def _resident(shape):
    nd = len(shape)
    return pl.BlockSpec(shape, lambda *_: (0,) * nd, pipeline_mode=pl.Buffered(1))


def _nbytes(shape, dtype):
    return int(np.prod(shape)) * jnp.dtype(dtype).itemsize


def _dot(a, b):
    return jnp.dot(a, b, preferred_element_type=F32)


def _dot_nt(a, b):
    return lax.dot_general(a, b, (((1,), (1,)), ((), ())), preferred_element_type=F32)


def _silu(x):
    return x * jax.nn.sigmoid(x)


def _rms(x, w):
    ms = jnp.mean(x * x, axis=-1, keepdims=True)
    return x * lax.rsqrt(ms + EPS) * w


def _row_mods(ml_ref, mc_ref, tile, tile_rows, n_ctx):
    rows = tile * tile_rows + lax.broadcasted_iota(jnp.int32, (tile_rows, 1), 0)
    is_ctx = rows < n_ctx

    def mod(k):
        return jnp.where(is_ctx, mc_ref[k:k + 1, :], ml_ref[0, k:k + 1, :])
    return mod


def _halo_maps(tile_rows, s_len):
    hb = tile_rows // HALO
    last = s_len // HALO - 1
    prev = lambda i, t: (i, jnp.maximum(t * hb - 1, 0), 0)
    nxt = lambda i, t: (i, jnp.minimum((t + 1) * hb, last), 0)
    return prev, nxt


def _seq_edges(tile, tile_rows, n_ctx, s_len):
    row0 = tile * tile_rows
    start = jnp.logical_or(row0 == 0, row0 == n_ctx)
    end = jnp.logical_or(row0 + tile_rows == n_ctx, row0 + tile_rows == s_len)
    return start, end


def _ada_kernel(c_ref, w_ref, b_ref, o_ref):
    s = _silu(c_ref[...]).astype(BF16)
    o_ref[0] = _dot(s, w_ref[0].astype(BF16)) + b_ref[0]


def _adaln(cc, w_ada, b_ada):
    depth, d, n = w_ada.shape
    rows = cc.shape[0]
    tn = n // 8
    est = 2 * (_nbytes((d, tn), F32) + _nbytes((rows, tn), F32)) + _nbytes((d, tn), BF16)
    return pl.pallas_call(
        _ada_kernel,
        grid=(depth, n // tn),
        in_specs=[pl.BlockSpec((rows, d), lambda i, j: (0, 0)),
                  pl.BlockSpec((1, d, tn), lambda i, j: (i, 0, j)),
                  pl.BlockSpec((1, 1, tn), lambda i, j: (i, 0, j))],
        out_specs=pl.BlockSpec((1, rows, tn), lambda i, j: (i, 0, j)),
        out_shape=jax.ShapeDtypeStruct((depth, rows, n), F32),
        compiler_params=_params(est, 2),
        name="adaln",
    )(cc, w_ada, b_ada.reshape(depth, 1, n))


def _ffn_body(h, mod, mod0, nw_ref, wg_ref, wu_ref, wd_ref, p_scr):
    d_ff = wg_ref.shape[1]
    a = (_rms(h, nw_ref[...]) * (1.0 + mod(mod0 + 1)) + mod(mod0)).astype(BF16)
    for j in range(d_ff // FFN_CHUNK):
        cols = slice(j * FFN_CHUNK, (j + 1) * FFN_CHUNK)
        g = _dot(a, wg_ref[:, cols])
        u = _dot(a, wu_ref[:, cols])
        p_scr[:, cols] = (_silu(g) * u).astype(BF16)
    y = _dot(p_scr[...], wd_ref[...])
    return h + (FFN_RES * mod(mod0 + 2)) * y


def _ffn_kernel(h_ref, ml_ref, mc_ref, nw_ref, wg_ref, wu_ref, wd_ref, o_ref, p_scr, *, n_ctx, mod0):
    tm = h_ref.shape[1]
    mod = _row_mods(ml_ref, mc_ref, pl.program_id(1), tm, n_ctx)
    o_ref[0] = _ffn_body(h_ref[0], mod, mod0, nw_ref, wg_ref, wu_ref, wd_ref, p_scr)


def _ffn_first_kernel(ctx_ref, x_ref, ml_ref, mc_ref, nw_ref, wg_ref, wu_ref, wd_ref, o_ref, p_scr, *, n_ctx, mod0):
    tm = x_ref.shape[1]
    t = pl.program_id(1)
    mod = _row_mods(ml_ref, mc_ref, t, tm, n_ctx)
    h = jnp.where(t < n_ctx // tm, ctx_ref[0], x_ref[0])
    o_ref[0] = _ffn_body(h, mod, mod0, nw_ref, wg_ref, wu_ref, wd_ref, p_scr)


def _ffn_last_kernel(h_ref, ml_ref, mc_ref, nw_ref, wg_ref, wu_ref, wd_ref, fw_ref, o_ref, p_scr, *, n_ctx, mod0):
    tm = h_ref.shape[1]
    mod = _row_mods(ml_ref, mc_ref, pl.program_id(1), tm, n_ctx)
    o_ref[0] = _rms(_ffn_body(h_ref[0], mod, mod0, nw_ref, wg_ref, wu_ref, wd_ref, p_scr), fw_ref[...])


def _ffn(h, ml, mc, norm_w, wg, wu, wd, *, n_ctx, mod0, ctx=None, final_w=None):
    b, _, d = h.shape
    d_ff = wg.shape[1]
    special = ctx is not None or final_w is not None
    tm = MIX_TILE if special else ROW_TILE
    s = h.shape[1] + (ctx.shape[1] if ctx is not None else 0)
    assert s % tm == 0 and d_ff % FFN_CHUNK == 0 and (not special or n_ctx % tm == 0)
    ctx_tiles = n_ctx // tm
    est = (4 * _nbytes((tm, d), F32) + 3 * _nbytes((d, d_ff), BF16) + _nbytes((tm, d_ff), BF16)
           + 4 * _nbytes((tm, FFN_CHUNK), F32) + 2 * _nbytes((tm, d), F32))
    tile = lambda i, t: (i, t, 0)
    lat_tile = lambda i, t: (i, jnp.maximum(t - ctx_tiles, 0), 0)
    common = [pl.BlockSpec((1, N_MOD, d), lambda i, t: (i, 0, 0)),
              _resident(mc.shape), _resident((1, d)),
              _resident(wg.shape), _resident(wu.shape), _resident(wd.shape)]
    args = [ml, mc, norm_w.reshape(1, d), wg, wu, wd]
    out_rows, out_spec = s, pl.BlockSpec((1, tm, d), tile)
    if ctx is not None:
        body = _ffn_first_kernel
        in_specs = [pl.BlockSpec((1, tm, d), lambda i, t: (i, jnp.minimum(t, ctx_tiles - 1), 0)),
                    pl.BlockSpec((1, tm, d), lat_tile)] + common
        args = [ctx, h] + args
    elif final_w is not None:
        body = _ffn_last_kernel
        in_specs = [pl.BlockSpec((1, tm, d), tile)] + common + [_resident((1, d))]
        args = [h] + args + [final_w.reshape(1, d)]
        out_rows, out_spec = s - n_ctx, pl.BlockSpec((1, tm, d), lat_tile)
    else:
        body = _ffn_kernel
        in_specs = [pl.BlockSpec((1, tm, d), tile)] + common
        args = [h] + args
    return pl.pallas_call(
        functools.partial(body, n_ctx=n_ctx, mod0=mod0),
        grid=(b, s // tm),
        in_specs=in_specs,
        out_specs=out_spec,
        out_shape=jax.ShapeDtypeStruct((b, out_rows, d), F32),
        scratch_shapes=[pltpu.VMEM((tm, d_ff), BF16)],
        compiler_params=_params(est, 2),
        name="ffn",
    )(*args)


def _proj_seq_kernel(h_ref, ml_ref, mc_ref, nw_ref, cos_ref, sin_ref, wz, wxbc, wdt, wqd, wkvd, wkr,
                     qnw_ref, kvnw_ref, wuqn, wuqp, wuqr, wuk, wuvt,
                     zs_o, xbc_o, dt_o, qn_o, qp_o, kn_o, kp_o, vt_o, *, n_ctx, qk_scale):
    tm = h_ref.shape[1]
    mod = _row_mods(ml_ref, mc_ref, pl.program_id(1), tm, n_ctx)
    a = (_rms(h_ref[0], nw_ref[...]) * (1.0 + mod(4)) + mod(3)).astype(BF16)
    zs_o[0] = _silu(_dot(a, wz[...])).astype(BF16)
    xbc_o[0] = _dot(a, wxbc[...]).astype(BF16)
    dt_o[0] = _dot(a, wdt[...])
    cos = cos_ref[...]
    sin = sin_ref[...]
    cos_h = jnp.concatenate([cos] * MLA_HEADS, axis=1)
    sin_h = jnp.concatenate([sin] * MLA_HEADS, axis=1)
    qn = _rms(_dot(a, wqd[...]), qnw_ref[...]).astype(BF16)
    qn_o[0] = (_dot(qn, wuqn[...]) * qk_scale).astype(BF16)
    qp_o[0] = ((_dot(qn, wuqp[...]) * cos_h + _dot(qn, wuqr[...]) * sin_h) * qk_scale).astype(BF16)
    ckv = _rms(_dot(a, wkvd[...]), kvnw_ref[...]).astype(BF16)
    kn_o[0] = _dot(ckv, wuk[...]).astype(BF16)
    vt_o[0] = _dot_nt(wuvt[...], ckv).astype(BF16)
    kk = _dot(a, wkr[...])
    kp_o[0] = (kk[:, :LANES] * cos + kk[:, LANES:] * sin).astype(BF16)


def _proj_seq(h, ml, mc, norm_w, cos_t, sin_t, w, *, n_ctx):
    b, s, d = h.shape
    tm = ROW_TILE
    assert s % tm == 0
    weights = [w["wz"], w["wxbc"], w["wdt"], w["wqd"], w["wkvd"], w["wkr"], w["q_norm"], w["kv_norm"],
               w["wuqn"], w["wuqp"], w["wuqr"], w["wuk"], w["wuvt"]]
    widths = [(SSD_INNER, BF16), (SSD_XBC, BF16), (LANES, F32), (MLA_HEADS * MLA_NOPE, BF16),
              (MLA_HEADS * LANES, BF16), (MLA_HEADS * MLA_NOPE, BF16), (LANES, BF16)]
    tile = lambda i, t: (i, t, 0)
    vw = MLA_HEADS * MLA_V
    est = (2 * _nbytes((tm, d), F32) + sum(_nbytes(x.shape, x.dtype) for x in weights)
           + 2 * sum(_nbytes((tm, n), dt) for n, dt in widths) + 2 * _nbytes((vw, tm), BF16)
           + 4 * _nbytes((tm, SSD_XBC), F32))
    return pl.pallas_call(
        functools.partial(_proj_seq_kernel, n_ctx=n_ctx, qk_scale=MLA_QK ** -0.5 * LOG2E),
        grid=(b, s // tm),
        in_specs=[pl.BlockSpec((1, tm, d), tile),
                  pl.BlockSpec((1, N_MOD, d), lambda i, t: (i, 0, 0)),
                  _resident(mc.shape), _resident((1, d)),
                  pl.BlockSpec((tm, LANES), lambda i, t: (t, 0)),
                  pl.BlockSpec((tm, LANES), lambda i, t: (t, 0))]
                 + [_resident(x.shape) for x in weights],
        out_specs=[pl.BlockSpec((1, tm, n), tile) for n, _ in widths]
                  + [pl.BlockSpec((1, vw, tm), lambda i, t: (i, 0, t))],
        out_shape=[jax.ShapeDtypeStruct((b, s, n), dt) for n, dt in widths]
                  + [jax.ShapeDtypeStruct((b, vw, s), BF16)],
        compiler_params=_params(est, 2),
        name="proj_seq",
    )(h, ml, mc, norm_w.reshape(1, d), cos_t, sin_t, *weights)


def _proj_loc_kernel(h_ref, ml_ref, mc_ref, nw_ref, wpv, wcv, wg, pv_o, cv_o, g_o, *, n_ctx):
    tm = h_ref.shape[1]
    mod = _row_mods(ml_ref, mc_ref, pl.program_id(1), tm, n_ctx)
    a = (_rms(h_ref[0], nw_ref[...]) * (1.0 + mod(4)) + mod(3)).astype(BF16)
    pv_o[0] = _dot(a, wpv[...]).astype(BF16)
    cv_o[0] = _dot(a, wcv[...]).astype(BF16)
    g_o[0] = jax.nn.sigmoid(_dot(a, wg[...])).astype(BF16)


def _proj_loc(h, ml, mc, norm_w, w, *, n_ctx):
    b, s, d = h.shape
    tm = ROW_TILE
    weights = [w["wpv"], w["wcv"], w["wg"]]
    widths = [x.shape[1] for x in weights]
    tile = lambda i, t: (i, t, 0)
    est = (2 * _nbytes((tm, d), F32) + sum(_nbytes(x.shape, BF16) for x in weights)
           + 2 * sum(_nbytes((tm, n), BF16) for n in widths) + 2 * _nbytes((tm, max(widths)), F32))
    return pl.pallas_call(
        functools.partial(_proj_loc_kernel, n_ctx=n_ctx),
        grid=(b, s // tm),
        in_specs=[pl.BlockSpec((1, tm, d), tile),
                  pl.BlockSpec((1, N_MOD, d), lambda i, t: (i, 0, 0)),
                  _resident(mc.shape), _resident((1, d))]
                 + [_resident(x.shape) for x in weights],
        out_specs=[pl.BlockSpec((1, tm, n), tile) for n in widths],
        out_shape=[jax.ShapeDtypeStruct((b, s, n), BF16) for n in widths],
        compiler_params=_params(est, 2),
        name="proj_loc",
    )(h, ml, mc, norm_w.reshape(1, d), *weights)


def _ssd_conv_kernel(x_ref, xp_ref, xn_ref, cw_ref, cb_ref, xs_o, c_o, bt_o, ext_s, *, n_ctx):
    tm = x_ref.shape[1]
    q = SSD_CHUNK
    n = SSD_STATE
    t = pl.program_id(1)
    seq_start, seq_end = _seq_edges(t, tm, n_ctx, pl.num_programs(1) * tm)
    ext_s[0:HALO, :] = jnp.where(seq_start, 0.0, xp_ref[0].astype(F32))
    ext_s[HALO:HALO + tm, :] = x_ref[0].astype(F32)
    ext_s[HALO + tm:, :] = jnp.where(seq_end, 0.0, xn_ref[0].astype(F32))
    pad_l = SSD_CONV // 2
    for j in range(SSD_XBC // LANES):
        lc = slice(j * LANES, (j + 1) * LANES)
        acc = cb_ref[:, lc] + cw_ref[0:1, lc] * ext_s[HALO - pad_l:HALO - pad_l + tm, lc]
        for k in range(1, SSD_CONV):
            off = HALO - pad_l + k
            acc = acc + cw_ref[k:k + 1, lc] * ext_s[off:off + tm, lc]
        act = _silu(acc)
        c0 = j * LANES
        if c0 < SSD_INNER:
            xs_o[0, :, lc] = act.astype(BF16)
        elif c0 < SSD_INNER + SSD_BC:
            g = (c0 - SSD_INNER) // n
            for ci in range(tm // q):
                r0 = (ci * SSD_GROUPS + g) * n
                bt_o[0, r0:r0 + n, :] = act[ci * q:(ci + 1) * q, :].T.astype(BF16)
        else:
            cc = c0 - SSD_INNER - SSD_BC
            c_o[0, :, cc:cc + LANES] = act.astype(BF16)


def _ssd_conv(xbc, conv_w, conv_b, *, n_ctx):
    b, s, _ = xbc.shape
    tm = MIX_TILE
    q = SSD_CHUNK
    assert s % tm == 0 and n_ctx % tm == 0 and tm % q == 0 and SSD_STATE == LANES
    tile = lambda i, t: (i, t, 0)
    prev, nxt = _halo_maps(tm, s)
    bt_rows = tm // q * SSD_BC
    est = (2 * _nbytes((tm + 2 * HALO, SSD_XBC), BF16) + _nbytes((tm + 2 * HALO, SSD_XBC), F32)
           + 2 * _nbytes((tm, SSD_XBC), BF16))
    return pl.pallas_call(
        functools.partial(_ssd_conv_kernel, n_ctx=n_ctx),
        grid=(b, s // tm),
        in_specs=[pl.BlockSpec((1, tm, SSD_XBC), tile),
                  pl.BlockSpec((1, HALO, SSD_XBC), prev),
                  pl.BlockSpec((1, HALO, SSD_XBC), nxt),
                  _resident(conv_w.shape), _resident((1, SSD_XBC))],
        out_specs=[pl.BlockSpec((1, tm, SSD_INNER), tile),
                   pl.BlockSpec((1, tm, SSD_BC), tile),
                   pl.BlockSpec((1, bt_rows, q), tile)],
        out_shape=[jax.ShapeDtypeStruct((b, s, SSD_INNER), BF16),
                   jax.ShapeDtypeStruct((b, s, SSD_BC), BF16),
                   jax.ShapeDtypeStruct((b, s // q * SSD_BC, q), BF16)],
        scratch_shapes=[pltpu.VMEM((tm + 2 * HALO, SSD_XBC), F32)],
        compiler_params=_params(est, 2),
        name="ssd_conv",
    )(xbc, xbc, xbc, conv_w, conv_b.reshape(1, -1))


def _split_bf16(x, parts):
    out = []
    for _ in range(parts):
        p = x.astype(BF16)
        out.append(p)
        x = x - p.astype(F32)
    return out


def _ssd_chunk(d, xs_ref, c_ref, bt_ref, dt_ref, y_ref, st_ref, bias_ref, alog_ref, dskip_ref, expand_ref, tri_ref):
    q = SSD_CHUNK
    n = SSD_STATE
    gw = HEADS_PER_GROUP * SSD_HEAD_DIM
    li = lax.broadcasted_iota(jnp.int32, (q, q), 0)
    si = lax.broadcasted_iota(jnp.int32, (q, q), 1)
    causal = (li >= si) if d == 0 else (li <= si)
    head_of_lane = lax.broadcasted_iota(jnp.int32, (q, gw), 1) // SSD_HEAD_DIM
    last = q - 1 if d == 0 else 0

    x = dt_ref[0] + bias_ref[...]
    dtv = jnp.maximum(x, 0.0) + jnp.log1p(jnp.exp(-jnp.abs(x)))
    da = dtv * (-LOG2E * jnp.exp(alog_ref[...]))
    tri = tri_ref[d]
    cs = sum(_dot(tri, p) for p in _split_bf16(da, 3))
    tot = cs[last:last + 1, :]
    cs_t = cs.T
    dt_t = dtv.T
    stack = jnp.concatenate(
        [dtv * jnp.exp2(tot - cs), jnp.exp2(cs), jnp.broadcast_to(jnp.exp2(tot), (BF16_ROWS, LANES))], axis=0)
    ex = _dot(stack.astype(BF16), expand_ref[d])
    w_exp = ex[:q]
    ecs_exp = ex[q:2 * q]
    etot_exp = ex[2 * q:2 * q + 1]
    xs = xs_ref[0]
    xe = (xs.astype(F32) * w_exp).astype(BF16)
    for g in range(SSD_GROUPS):
        gc = slice(g * gw, (g + 1) * gw)
        c_g = c_ref[0, :, g * n:(g + 1) * n]
        bt_g = bt_ref[0, g * n:(g + 1) * n, :]
        cb = _dot(c_g, bt_g)
        st_g = st_ref[:, gc]
        y_off = _dot(c_g, st_g.astype(BF16))
        xs_g = xs[:, gc]
        m_parts, x_parts = [], []
        for r in range(HEADS_PER_GROUP):
            col = d * SSD_HEADS + g * HEADS_PER_GROUP + r
            seg = cs[:, col:col + 1] - cs_t[col:col + 1, :]
            decay = jnp.exp2(jnp.where(causal, seg, -jnp.inf))
            m_parts.append((cb * decay * dt_t[col:col + 1, :]).astype(BF16))
            x_parts.append(jnp.where(head_of_lane == r, xs_g, jnp.zeros_like(xs_g)))
        y_g = _dot(jnp.concatenate(m_parts, axis=1), jnp.concatenate(x_parts, axis=0))
        y_g = y_g + y_off * ecs_exp[:, gc]
        if d == 0:
            y_g = y_g + xs_g.astype(F32) * dskip_ref[:, gc]
        y_ref[0, :, gc] = y_g.astype(BF16)
        st_ref[:, gc] = st_g * etot_exp[:, gc] + _dot(bt_g, xe[:, gc])


def _ssd_scan_kernel(xs_f, c_f, bt_f, dt_f, xs_b, c_b, bt_b, dt_b, bias_ref, alog_ref, dskip_ref, expand_ref,
                     tri_ref, yf_ref, yb_ref, stf_s, stb_s):
    @pl.when(pl.program_id(1) == 0)
    def _reset():
        stf_s[...] = jnp.zeros(stf_s.shape, F32)
        stb_s[...] = jnp.zeros(stb_s.shape, F32)
    shared = (bias_ref, alog_ref, dskip_ref, expand_ref, tri_ref)
    _ssd_chunk(0, xs_f, c_f, bt_f, dt_f, yf_ref, stf_s, *shared)
    _ssd_chunk(1, xs_b, c_b, bt_b, dt_b, yb_ref, stb_s, *shared)


def _ssd_scan(xs, cm, bt, dt, dt_bias, a_log, d_skip, *, n_ctx):
    b, s, _ = xs.shape
    q = SSD_CHUNK
    assert s % q == 0 and n_ctx % q == 0
    nc = s // q
    nct = n_ctx // q
    pad = LANES - 2 * SSD_HEADS
    bias_row = jnp.pad(dt_bias.reshape(1, -1), ((0, 0), (0, pad)))
    alog_row = jnp.pad(a_log.reshape(1, -1), ((0, 0), (0, pad)))
    dskip_row = jnp.repeat(d_skip, SSD_HEAD_DIM).reshape(1, SSD_INNER)
    e = np.zeros((2, LANES, SSD_INNER), np.float32)
    for d in range(2):
        for h in range(SSD_HEADS):
            e[d, d * SSD_HEADS + h, h * SSD_HEAD_DIM:(h + 1) * SSD_HEAD_DIM] = 1.0
    low = np.tril(np.ones((q, q), np.float32))
    tri = np.stack([low, low.T])
    fwd = lambda i, c: (i, c, 0)
    bwd = lambda i, c: (i, jnp.where(c < nct, nct - 1 - c, nc - 1 - c + nct), 0)
    per_dir = lambda m: [pl.BlockSpec((1, q, SSD_INNER), m), pl.BlockSpec((1, q, SSD_BC), m),
                         pl.BlockSpec((1, SSD_BC, q), m), pl.BlockSpec((1, q, LANES), m)]
    est = (4 * (_nbytes((q, SSD_INNER + SSD_BC), BF16) + _nbytes((SSD_BC, q), BF16) + _nbytes((q, LANES), F32))
           + 4 * _nbytes((q, SSD_INNER), BF16) + 2 * _nbytes((SSD_STATE, SSD_INNER), F32)
           + _nbytes(e.shape, BF16) + 12 * _nbytes((q, SSD_INNER), F32))
    return pl.pallas_call(
        _ssd_scan_kernel,
        grid=(b, nc),
        in_specs=per_dir(fwd) + per_dir(bwd)
                 + [_resident((1, LANES)), _resident((1, LANES)), _resident((1, SSD_INNER)),
                    _resident(e.shape), _resident(tri.shape)],
        out_specs=[pl.BlockSpec((1, q, SSD_INNER), fwd), pl.BlockSpec((1, q, SSD_INNER), bwd)],
        out_shape=[jax.ShapeDtypeStruct((b, s, SSD_INNER), BF16)] * 2,
        scratch_shapes=[pltpu.VMEM((SSD_STATE, SSD_INNER), F32)] * 2,
        compiler_params=_params(est, 2),
        name="ssd_scan",
    )(xs, cm, bt, dt, xs, cm, bt, dt, bias_row, alog_row, dskip_row, jnp.asarray(e, BF16), jnp.asarray(tri, BF16))


def _attn_kernel(*refs, q_blocks):
    qn_refs, qp_refs = refs[:q_blocks], refs[q_blocks:2 * q_blocks]
    kn_ref, kp_ref, vt_ref, o_ref = refs[2 * q_blocks:]
    kp = kp_ref[0]
    for h in range(MLA_HEADS):
        hc = slice(h * LANES, (h + 1) * LANES)
        qc = jnp.concatenate(
            [jnp.concatenate([qn[0, :, hc], qp[0, :, hc]], axis=1) for qn, qp in zip(qn_refs, qp_refs)], axis=0)
        kc = jnp.concatenate([kn_ref[0, :, hc], kp], axis=1)
        st = _dot_nt(kc, qc)
        p = jnp.exp2(st - jnp.max(st, axis=0, keepdims=True))
        denom = jnp.sum(p, axis=0, keepdims=True)
        ot = _dot(vt_ref[0, hc, :], p.astype(BF16))
        o_ref[0, :, hc] = (ot / denom).T.astype(BF16)


def _attention(qn, qp, kn, kp, vt, *, n_ctx):
    b, s, w = qn.shape
    tb = MIX_TILE
    lat_q_blocks = 2
    tq = tb * lat_q_blocks
    assert n_ctx == tb and (s - n_ctx) % tq == 0 and MLA_V == LANES
    ctx_blocks = n_ctx // tb

    def call(q_blocks, n_keys, q_maps, n_tiles, name):
        rows = q_blocks * tb
        whole = lambda i, t: (i, 0, 0)
        est = (4 * _nbytes((rows, w), BF16) + 2 * (2 * _nbytes((n_keys, w), BF16) + _nbytes((n_keys, LANES), BF16))
               + 2 * _nbytes((rows, w), BF16) + 3 * _nbytes((n_keys, rows), F32) + _nbytes((n_keys, 2 * LANES), BF16))
        return pl.pallas_call(
            functools.partial(_attn_kernel, q_blocks=q_blocks),
            grid=(b, n_tiles),
            in_specs=[pl.BlockSpec((1, tb, w), m) for m in q_maps] * 2
                     + [pl.BlockSpec((1, n_keys, w), whole), pl.BlockSpec((1, n_keys, LANES), whole),
                        pl.BlockSpec((1, w, n_keys), whole)],
            out_specs=pl.BlockSpec((1, rows, w), lambda i, t: (i, t, 0)),
            out_shape=jax.ShapeDtypeStruct((b, n_tiles * rows, w), BF16),
            compiler_params=_params(est, 2),
            name=name,
        )(*([qn] * q_blocks), *([qp] * q_blocks), kn, kp, vt)

    att_c = call(1, n_ctx, [lambda i, t: (i, 0, 0)], 1, "attention_ctx")
    lat_maps = [lambda i, t, j=j: (i, ctx_blocks + lat_q_blocks * t + j, 0) for j in range(lat_q_blocks)]
    att_l = call(lat_q_blocks, s, lat_maps, (s - n_ctx) // tq, "attention")
    return att_c, att_l


def _pool_bands(tile_rows):
    t = np.arange(tile_rows)[:, None]
    j = np.arange(tile_rows + 2 * HALO)[None, :]
    bands = []
    for win in POOL_WINDOWS:
        lo = HALO + t - win // 2
        bands.append(((j >= lo) & (j < lo + win)).astype(np.float32))
    return np.stack(bands)


def _merge_kernel(h_ref, ml_ref, mc_ref, yf_ref, yb_ref, zs_ref, attc_ref, attl_ref,
                  pv_ref, pvp_ref, pvn_ref, cv_ref, cvp_ref, cvn_ref, sg_ref,
                  snw_ref, wso, wmo, band_ref, pw_ref, ps_ref, wpo, cw_ref, wco, wo,
                  o_ref, ext_u, *, n_ctx):
    tm = h_ref.shape[1]
    d = h_ref.shape[2]
    t = pl.program_id(1)
    s_len = pl.num_programs(1) * tm
    mod = _row_mods(ml_ref, mc_ref, t, tm, n_ctx)
    seq_start, seq_end = _seq_edges(t, tm, n_ctx, s_len)

    y = yf_ref[0].astype(F32) + yb_ref[0].astype(F32)
    ssd_b = _dot(_rms(y * zs_ref[0].astype(F32), snw_ref[...]).astype(BF16), wso[...])

    mla_b = _dot(jnp.where(t < n_ctx // tm, attc_ref[0], attl_ref[0]), wmo[...])

    pv = pv_ref[0]
    ext_p = jnp.concatenate([jnp.where(seq_start, jnp.zeros_like(pvp_ref[0]), pvp_ref[0]), pv,
                             jnp.where(seq_end, jnp.zeros_like(pvn_ref[0]), pvn_ref[0])], axis=0)
    rows = t * tm + lax.broadcasted_iota(jnp.int32, (tm, 1), 0)
    in_ctx = rows < n_ctx
    pos = jnp.where(in_ctx, rows, rows - n_ctx)
    seq_n = jnp.where(in_ctx, n_ctx, s_len - n_ctx)
    pooled = []
    for gi, win in enumerate(POOL_WINDOWS):
        gc = slice(gi * POOL_GROUP, (gi + 1) * POOL_GROUP)
        left = win // 2
        lo = jnp.clip(pos - left, 0, seq_n)
        hi = jnp.clip(pos - left + win, 0, seq_n)
        p_g = _dot(band_ref[gi], ext_p[:, gc]) / (hi - lo).astype(F32) - pv[:, gc].astype(F32)
        pooled.append(_dot(p_g.astype(BF16), pw_ref[gi]))
    pool_y = jnp.concatenate(pooled, axis=1) * ps_ref[...]
    pool_b = _dot(pool_y.astype(BF16), wpo[...])

    def gated_in(ref):
        return ref[0, :, CONV_WIDTH:2 * CONV_WIDTH].astype(F32) * ref[0, :, 2 * CONV_WIDTH:].astype(F32)
    ext_u[0:HALO, :] = jnp.where(seq_start, 0.0, gated_in(cvp_ref))
    ext_u[HALO:HALO + tm, :] = gated_in(cv_ref)
    ext_u[HALO + tm:, :] = jnp.where(seq_end, 0.0, gated_in(cvn_ref))
    conv = cw_ref[0:1, :] * ext_u[HALO - 1:HALO - 1 + tm, :]
    for k in range(1, CONV_K):
        conv = conv + cw_ref[k:k + 1, :] * ext_u[HALO - 1 + k:HALO - 1 + k + tm, :]
    conv_b = _dot((cv_ref[0, :, :CONV_WIDTH].astype(F32) * conv).astype(BF16), wco[...])

    merged = sg_ref[0, :, 0:d].astype(F32) * ssd_b
    for k, br in enumerate((mla_b, pool_b, conv_b), start=1):
        merged = merged + sg_ref[0, :, k * d:(k + 1) * d].astype(F32) * br
    o_ref[0] = h_ref[0] + mod(5) * _dot(merged.astype(BF16), wo[...])


def _merge(h, ml, mc, yf, yb, zs, att_c, att_l, pv, cv, sg, w, *, n_ctx):
    b, s, d = h.shape
    tm = MIX_TILE
    assert s % tm == 0 and n_ctx % tm == 0 and tm % HALO == 0
    ctx_tiles = n_ctx // tm
    tile = lambda i, t: (i, t, 0)
    prev, nxt = _halo_maps(tm, s)
    bands = _pool_bands(tm)
    weights = [w["ssd_norm"], w["ssd_w_out"], w["mla_w_out"], jnp.asarray(bands, BF16), w["pool_w"],
               w["pool_scale"], w["pool_w_out"], w["sconv_w"], w["sconv_w_out"], w["w_o"]]
    stream = (2 * _nbytes((tm, d), F32) + _nbytes((tm, 4 * SSD_INNER + POOL_WIDTH + 3 * CONV_WIDTH + N_BRANCH * d), BF16)
              + 2 * _nbytes((HALO, POOL_WIDTH + 3 * CONV_WIDTH), BF16))
    est = (2 * stream + sum(_nbytes(x.shape, x.dtype) for x in weights)
           + _nbytes((tm + 2 * HALO, d), F32) + 10 * _nbytes((tm, d), F32))
    return pl.pallas_call(
        functools.partial(_merge_kernel, n_ctx=n_ctx),
        grid=(b, s // tm),
        in_specs=[pl.BlockSpec((1, tm, d), tile),
                  pl.BlockSpec((1, N_MOD, d), lambda i, t: (i, 0, 0)),
                  _resident(mc.shape),
                  pl.BlockSpec((1, tm, SSD_INNER), tile),
                  pl.BlockSpec((1, tm, SSD_INNER), tile),
                  pl.BlockSpec((1, tm, SSD_INNER), tile),
                  pl.BlockSpec((1, tm, MLA_HEADS * MLA_V), lambda i, t: (i, jnp.minimum(t, ctx_tiles - 1), 0)),
                  pl.BlockSpec((1, tm, MLA_HEADS * MLA_V), lambda i, t: (i, jnp.maximum(t - ctx_tiles, 0), 0)),
                  pl.BlockSpec((1, tm, POOL_WIDTH), tile),
                  pl.BlockSpec((1, HALO, POOL_WIDTH), prev),
                  pl.BlockSpec((1, HALO, POOL_WIDTH), nxt),
                  pl.BlockSpec((1, tm, 3 * CONV_WIDTH), tile),
                  pl.BlockSpec((1, HALO, 3 * CONV_WIDTH), prev),
                  pl.BlockSpec((1, HALO, 3 * CONV_WIDTH), nxt),
                  pl.BlockSpec((1, tm, N_BRANCH * d), tile)]
                 + [_resident(x.shape) for x in weights],
        out_specs=pl.BlockSpec((1, tm, d), tile),
        out_shape=jax.ShapeDtypeStruct(h.shape, F32),
        scratch_shapes=[pltpu.VMEM((tm + 2 * HALO, CONV_WIDTH), F32)],
        compiler_params=_params(est, 2),
        name="merge",
    )(h, ml, mc, yf, yb, zs, att_c, att_l, pv, pv, pv, cv, cv, cv, sg, *weights)


def _rope_tables(n_ctx, n_lat):
    rows = n_lat // GRID_W
    row = np.repeat(np.arange(rows), GRID_W).astype(np.float32)
    col = np.tile(np.arange(GRID_W), rows).astype(np.float32)
    half = MLA_ROPE // 2
    inv = (1.0 / (ROPE_THETA ** (jnp.arange(0, half, 2, dtype=F32) / half)))
    ar = jnp.asarray(row)[:, None] * inv
    ac = jnp.asarray(col)[:, None] * inv
    ang = jnp.concatenate([ar, ar, ac, ac], axis=-1)
    pad = LANES - MLA_ROPE
    cos = jnp.concatenate([jnp.ones((n_ctx, MLA_ROPE), F32), jnp.cos(ang)], axis=0)
    sin = jnp.concatenate([jnp.zeros((n_ctx, MLA_ROPE), F32), jnp.sin(ang)], axis=0)
    return jnp.pad(cos, ((0, 0), (0, pad))), jnp.pad(sin, ((0, 0), (0, pad)))


def _rot_cols(w):
    w1, w2, w3, w4 = jnp.split(w, 4, axis=-1)
    return jnp.concatenate([-w2, w1, -w4, w3], axis=-1)


def _layer_weights(i, p):
    w_in = p["w_in"][i]
    splits = np.cumsum([SSD_INNER, SSD_XBC, 2 * SSD_HEADS, MLA_Q_RANK, MLA_KV_RANK, MLA_ROPE,
                        POOL_WIDTH, 3 * CONV_WIDTH])
    wz, wxbc, wdt, wqd, wkvd, wkr, wpv, wcv, wg = jnp.split(w_in, splits, axis=1)
    lane_pad = lambda x: jnp.pad(x, ((0, 0), (0, LANES - x.shape[1])))
    uq = p["mla_w_uq"][i].reshape(MLA_Q_RANK, MLA_HEADS, MLA_QK)
    uq_pe = uq[:, :, MLA_NOPE:]
    head_pad = lambda x: jnp.pad(x, ((0, 0), (0, 0), (0, LANES - MLA_ROPE))).reshape(MLA_Q_RANK, MLA_HEADS * LANES)
    bf = lambda x: x.astype(BF16)
    return {
        "wz": bf(wz), "wxbc": bf(wxbc), "wdt": bf(lane_pad(wdt)), "wqd": bf(wqd), "wkvd": bf(wkvd),
        "wkr": bf(jnp.concatenate([lane_pad(wkr), lane_pad(_rot_cols(wkr))], axis=1)),
        "q_norm": p["mla_q_norm"][i].reshape(1, -1), "kv_norm": p["mla_kv_norm"][i].reshape(1, -1),
        "wuqn": bf(uq[:, :, :MLA_NOPE].reshape(MLA_Q_RANK, MLA_HEADS * MLA_NOPE)),
        "wuqp": bf(head_pad(uq_pe)), "wuqr": bf(head_pad(_rot_cols(uq_pe))),
        "wuk": bf(p["mla_w_uk"][i]), "wuvt": bf(p["mla_w_uv"][i].T),
        "wpv": bf(wpv), "wcv": bf(wcv), "wg": bf(wg),
        "ssd_norm": p["ssd_norm"][i].reshape(1, -1), "ssd_w_out": bf(p["ssd_w_out"][i]),
        "mla_w_out": bf(p["mla_w_out"][i]), "pool_w": bf(p["pool_w"][i]),
        "pool_scale": p["pool_scale"][i].reshape(1, -1), "pool_w_out": bf(p["pool_w_out"][i]),
        "sconv_w": p["sconv_w"][i], "sconv_w_out": bf(p["sconv_w_out"][i]), "w_o": bf(p["w_o"][i]),
    }


def kernel(x, c, ctx, c_ctx, w_ada, b_ada, ffn1_norm, ffn1_w_gate, ffn1_w_up, ffn1_w_down, mix_norm, w_in, ssd_conv_w, ssd_conv_b, ssd_dt_bias, ssd_a_log, ssd_d, ssd_norm, ssd_w_out, mla_q_norm, mla_w_uq, mla_kv_norm, mla_w_uk, mla_w_uv, mla_w_out, pool_w, pool_scale, pool_w_out, sconv_w, sconv_w_out, w_o, ffn2_norm, ffn2_w_gate, ffn2_w_up, ffn2_w_down, final_norm):
    p = dict(w_in=w_in, ssd_norm=ssd_norm, ssd_w_out=ssd_w_out, mla_q_norm=mla_q_norm, mla_w_uq=mla_w_uq,
             mla_kv_norm=mla_kv_norm, mla_w_uk=mla_w_uk, mla_w_uv=mla_w_uv, mla_w_out=mla_w_out,
             pool_w=pool_w, pool_scale=pool_scale, pool_w_out=pool_w_out, sconv_w=sconv_w,
             sconv_w_out=sconv_w_out, w_o=w_o)
    bsz, n_lat, d = x.shape
    n_ctx = ctx.shape[1]
    depth = w_ada.shape[0]

    mod_rows = -(-(bsz + 1) // 8) * 8
    cc = jnp.concatenate([c, c_ctx[None, :], jnp.zeros((mod_rows - bsz - 1, d), F32)], axis=0)
    mods = _adaln(cc, w_ada, b_ada).reshape(depth, mod_rows, N_MOD, d)

    cos_t, sin_t = _rope_tables(n_ctx, n_lat)
    h = x
    for i in range(depth):
        ml, mc = mods[i, :bsz], mods[i, bsz]
        w = _layer_weights(i, p)
        bf = lambda a: a.astype(BF16)
        h = _ffn(h, ml, mc, ffn1_norm[i], bf(ffn1_w_gate[i]), bf(ffn1_w_up[i]), bf(ffn1_w_down[i]),
                 n_ctx=n_ctx, mod0=0, ctx=ctx if i == 0 else None)
        zs, xbc, dt, qn, qp, kn, kp, vt = _proj_seq(h, ml, mc, mix_norm[i], cos_t, sin_t, w, n_ctx=n_ctx)
        pv, cv, sg = _proj_loc(h, ml, mc, mix_norm[i], w, n_ctx=n_ctx)
        xs, cm, bt = _ssd_conv(xbc, ssd_conv_w[i], ssd_conv_b[i], n_ctx=n_ctx)
        yf, yb = _ssd_scan(xs, cm, bt, dt, ssd_dt_bias[i], ssd_a_log[i], ssd_d[i], n_ctx=n_ctx)
        att_c, att_l = _attention(qn, qp, kn, kp, vt, n_ctx=n_ctx)
        h = _merge(h, ml, mc, yf, yb, zs, att_c, att_l, pv, cv, sg, w, n_ctx=n_ctx)
        h = _ffn(h, ml, mc, ffn2_norm[i], bf(ffn2_w_gate[i]), bf(ffn2_w_up[i]), bf(ffn2_w_down[i]),
                 n_ctx=n_ctx, mod0=6, final_w=final_norm if i == depth - 1 else None)
    return h
```

```python
import functools
import math

import jax
import jax.numpy as jnp
import numpy as np
from jax import lax
from jax.experimental import pallas as pl
from jax.experimental.pallas import tpu as pltpu

F32 = jnp.float32
BF16 = jnp.bfloat16

EPS = 1e-6
LOG2E = math.log2(math.e)
FFN_RES = 0.5
N_MOD = 9
GRID_W = 64
ROPE_THETA = 10000.0

SSD_HEADS = 16
SSD_HEAD_DIM = 64
SSD_GROUPS = 4
SSD_STATE = 128
SSD_CONV = 4
SSD_CHUNK = 128
SSD_INNER = SSD_HEADS * SSD_HEAD_DIM
SSD_BC = SSD_GROUPS * SSD_STATE
SSD_XBC = SSD_INNER + 2 * SSD_BC
HEADS_PER_GROUP = SSD_HEADS // SSD_GROUPS

MLA_HEADS = 8
MLA_Q_RANK = 384
MLA_KV_RANK = 256
MLA_NOPE = 128
MLA_ROPE = 64
MLA_V = 128
MLA_QK = MLA_NOPE + MLA_ROPE

POOL_WINDOWS = (2, 4, 8, 16)
POOL_GROUP = 256
POOL_WIDTH = POOL_GROUP * len(POOL_WINDOWS)
CONV_WIDTH = 1024
CONV_K = 3
N_BRANCH = 4

LANES = 128
BF16_ROWS = 16
HALO = BF16_ROWS
VMEM_CAP = 56 * 1024 * 1024
VMEM_SLACK = 12 * 1024 * 1024

ROW_TILE = 768
MIX_TILE = 256
FFN_CHUNK = 256
ATTN_KEY_BLOCK = 256


def _params(est_bytes, n_axes):
    return pltpu.CompilerParams(
        dimension_semantics=("arbitrary",) * n_axes,
        vmem_limit_bytes=int(min(est_bytes + VMEM_SLACK, VMEM_CAP)))


def _resident(shape):
    nd = len(shape)
    return pl.BlockSpec(shape, lambda *_: (0,) * nd, pipeline_mode=pl.Buffered(1))


def _nbytes(shape, dtype):
    return int(np.prod(shape)) * jnp.dtype(dtype).itemsize


def _dot(a, b):
    return jnp.dot(a, b, preferred_element_type=F32)


def _dot_nt(a, b):
    return lax.dot_general(a, b, (((1,), (1,)), ((), ())), preferred_element_type=F32)


def _silu(x):
    return x * jax.nn.sigmoid(x)


def _rms(x, w):
    ms = jnp.mean(x * x, axis=-1, keepdims=True)
    return x * lax.rsqrt(ms + EPS) * w


def _row_mods(ml_ref, mc_ref, tile, tile_rows, n_ctx):
    rows = tile * tile_rows + lax.broadcasted_iota(jnp.int32, (tile_rows, 1), 0)
    is_ctx = rows < n_ctx

    def mod(k):
        return jnp.where(is_ctx, mc_ref[k:k + 1, :], ml_ref[0, k:k + 1, :])
    return mod


def _halo_maps(tile_rows, s_len):
    hb = tile_rows // HALO
    last = s_len // HALO - 1
    prev = lambda i, t: (i, jnp.maximum(t * hb - 1, 0), 0)
    nxt = lambda i, t: (i, jnp.minimum((t + 1) * hb, last), 0)
    return prev, nxt


def _seq_edges(tile, tile_rows, n_ctx, s_len):
    row0 = tile * tile_rows
    start = jnp.logical_or(row0 == 0, row0 == n_ctx)
    end = jnp.logical_or(row0 + tile_rows == n_ctx, row0 + tile_rows == s_len)
    return start, end


def _ada_kernel(c_ref, w_ref, b_ref, o_ref):
    s = _silu(c_ref[...]).astype(BF16)
    o_ref[0] = _dot(s, w_ref[0].astype(BF16)) + b_ref[0]


def _adaln(cc, w_ada, b_ada):
    depth, d, n = w_ada.shape
    rows = cc.shape[0]
    tn = n // 8
    est = 2 * (_nbytes((d, tn), F32) + _nbytes((rows, tn), F32)) + _nbytes((d, tn), BF16)
    return pl.pallas_call(
        _ada_kernel,
        grid=(depth, n // tn),
        in_specs=[pl.BlockSpec((rows, d), lambda i, j: (0, 0)),
                  pl.BlockSpec((1, d, tn), lambda i, j: (i, 0, j)),
                  pl.BlockSpec((1, 1, tn), lambda i, j: (i, 0, j))],
        out_specs=pl.BlockSpec((1, rows, tn), lambda i, j: (i, 0, j)),
        out_shape=jax.ShapeDtypeStruct((depth, rows, n), F32),
        compiler_params=_params(est, 2),
        name="adaln",
    )(cc, w_ada, b_ada.reshape(depth, 1, n))


def _ffn_body(h, mod, mod0, nw_ref, wg_ref, wu_ref, wd_ref, p_scr):
    d_ff = wg_ref.shape[1]
    a = (_rms(h, nw_ref[...]) * (1.0 + mod(mod0 + 1)) + mod(mod0)).astype(BF16)
    for j in range(d_ff // FFN_CHUNK):
        cols = slice(j * FFN_CHUNK, (j + 1) * FFN_CHUNK)
        g = _dot(a, wg_ref[:, cols])
        u = _dot(a, wu_ref[:, cols])
        p_scr[:, cols] = (_silu(g) * u).astype(BF16)
    y = _dot(p_scr[...], wd_ref[...])
    return h + (FFN_RES * mod(mod0 + 2)) * y


def _ffn_kernel(h_ref, ml_ref, mc_ref, nw_ref, wg_ref, wu_ref, wd_ref, o_ref, p_scr, *, n_ctx, mod0):
    tm = h_ref.shape[1]
    mod = _row_mods(ml_ref, mc_ref, pl.program_id(1), tm, n_ctx)
    o_ref[0] = _ffn_body(h_ref[0], mod, mod0, nw_ref, wg_ref, wu_ref, wd_ref, p_scr)


def _ffn_first_kernel(ctx_ref, x_ref, ml_ref, mc_ref, nw_ref, wg_ref, wu_ref, wd_ref, o_ref, p_scr, *, n_ctx, mod0):
    tm = x_ref.shape[1]
    t = pl.program_id(1)
    mod = _row_mods(ml_ref, mc_ref, t, tm, n_ctx)
    h = jnp.where(t < n_ctx // tm, ctx_ref[0], x_ref[0])
    o_ref[0] = _ffn_body(h, mod, mod0, nw_ref, wg_ref, wu_ref, wd_ref, p_scr)


def _ffn_last_kernel(h_ref, ml_ref, mc_ref, nw_ref, wg_ref, wu_ref, wd_ref, fw_ref, o_ref, p_scr, *, n_ctx, mod0):
    tm = h_ref.shape[1]
    mod = _row_mods(ml_ref, mc_ref, pl.program_id(1), tm, n_ctx)
    o_ref[0] = _rms(_ffn_body(h_ref[0], mod, mod0, nw_ref, wg_ref, wu_ref, wd_ref, p_scr), fw_ref[...])


def _ffn(h, ml, mc, norm_w, wg, wu, wd, *, n_ctx, mod0, ctx=None, final_w=None):
    b, _, d = h.shape
    d_ff = wg.shape[1]
    special = ctx is not None or final_w is not None
    tm = MIX_TILE if special else ROW_TILE
    s = h.shape[1] + (ctx.shape[1] if ctx is not None else 0)
    assert s % tm == 0 and d_ff % FFN_CHUNK == 0 and (not special or n_ctx % tm == 0)
    ctx_tiles = n_ctx // tm
    est = (4 * _nbytes((tm, d), F32) + 3 * _nbytes((d, d_ff), BF16) + _nbytes((tm, d_ff), BF16)
           + 4 * _nbytes((tm, FFN_CHUNK), F32) + 2 * _nbytes((tm, d), F32))
    tile = lambda i, t: (i, t, 0)
    lat_tile = lambda i, t: (i, jnp.maximum(t - ctx_tiles, 0), 0)
    common = [pl.BlockSpec((1, N_MOD, d), lambda i, t: (i, 0, 0)),
              _resident(mc.shape), _resident((1, d)),
              _resident(wg.shape), _resident(wu.shape), _resident(wd.shape)]
    args = [ml, mc, norm_w.reshape(1, d), wg, wu, wd]
    out_rows, out_spec = s, pl.BlockSpec((1, tm, d), tile)
    if ctx is not None:
        body = _ffn_first_kernel
        in_specs = [pl.BlockSpec((1, tm, d), lambda i, t: (i, jnp.minimum(t, ctx_tiles - 1), 0)),
                    pl.BlockSpec((1, tm, d), lat_tile)] + common
        args = [ctx, h] + args
    elif final_w is not None:
        body = _ffn_last_kernel
        in_specs = [pl.BlockSpec((1, tm, d), tile)] + common + [_resident((1, d))]
        args = [h] + args + [final_w.reshape(1, d)]
        out_rows, out_spec = s - n_ctx, pl.BlockSpec((1, tm, d), lat_tile)
    else:
        body = _ffn_kernel
        in_specs = [pl.BlockSpec((1, tm, d), tile)] + common
        args = [h] + args
    return pl.pallas_call(
        functools.partial(body, n_ctx=n_ctx, mod0=mod0),
        grid=(b, s // tm),
        in_specs=in_specs,
        out_specs=out_spec,
        out_shape=jax.ShapeDtypeStruct((b, out_rows, d), F32),
        scratch_shapes=[pltpu.VMEM((tm, d_ff), BF16)],
        compiler_params=_params(est, 2),
        name="ffn",
    )(*args)


def _proj_seq_kernel(h_ref, ml_ref, mc_ref, nw_ref, cos_ref, sin_ref, wz, wxbc, wsm,
                     qnw_ref, kvnw_ref, wuqn, wuqp, wuqr, wuk, wuvt,
                     zs_o, xbc_o, dt_o, qn_o, qp_o, kn_o, kp_o, vt_o, *, n_ctx, qk_scale):
    tm = h_ref.shape[1]
    mod = _row_mods(ml_ref, mc_ref, pl.program_id(1), tm, n_ctx)
    a = (_rms(h_ref[0], nw_ref[...]) * (1.0 + mod(4)) + mod(3)).astype(BF16)
    zs_o[0] = _silu(_dot(a, wz[...])).astype(BF16)
    xbc_o[0] = _dot(a, wxbc[...]).astype(BF16)
    sm = _dot(a, wsm[...])
    c0 = MLA_Q_RANK + MLA_KV_RANK
    qd, kvd, kr, kr_rot = sm[:, :MLA_Q_RANK], sm[:, MLA_Q_RANK:c0], sm[:, c0:c0 + LANES], sm[:, c0 + LANES:c0 + 2 * LANES]
    dt_o[0] = sm[:, c0 + 2 * LANES:]
    cos = cos_ref[...]
    sin = sin_ref[...]
    cos_h = jnp.concatenate([cos] * MLA_HEADS, axis=1)
    sin_h = jnp.concatenate([sin] * MLA_HEADS, axis=1)
    qn = _rms(qd, qnw_ref[...]).astype(BF16)
    qn_o[0] = (_dot(qn, wuqn[...]) * qk_scale).astype(BF16)
    qp_o[0] = ((_dot(qn, wuqp[...]) * cos_h + _dot(qn, wuqr[...]) * sin_h) * qk_scale).astype(BF16)
    ckv = _rms(kvd, kvnw_ref[...]).astype(BF16)
    kn_o[0] = _dot(ckv, wuk[...]).astype(BF16)
    vt_o[0] = _dot_nt(wuvt[...], ckv).astype(BF16)
    kp_o[0] = (kr * cos + kr_rot * sin).astype(BF16)


def _proj_seq(h, ml, mc, norm_w, cos_t, sin_t, w, *, n_ctx):
    b, s, d = h.shape
    tm = ROW_TILE
    assert s % tm == 0
    assert MLA_Q_RANK % LANES == 0 and MLA_KV_RANK % LANES == 0
    weights = [w["wz"], w["wxbc"], w["wsm"], w["q_norm"], w["kv_norm"],
               w["wuqn"], w["wuqp"], w["wuqr"], w["wuk"], w["wuvt"]]
    widths = [(SSD_INNER, BF16), (SSD_XBC, BF16), (LANES, F32), (MLA_HEADS * MLA_NOPE, BF16),
              (MLA_HEADS * LANES, BF16), (MLA_HEADS * MLA_NOPE, BF16), (LANES, BF16)]
    tile = lambda i, t: (i, t, 0)
    vw = MLA_HEADS * MLA_V
    est = (2 * _nbytes((tm, d), F32) + sum(_nbytes(x.shape, x.dtype) for x in weights)
           + 2 * sum(_nbytes((tm, n), dt) for n, dt in widths) + 2 * _nbytes((vw, tm), BF16)
           + 4 * _nbytes((tm, SSD_XBC), F32))
    return pl.pallas_call(
        functools.partial(_proj_seq_kernel, n_ctx=n_ctx, qk_scale=MLA_QK ** -0.5 * LOG2E),
        grid=(b, s // tm),
        in_specs=[pl.BlockSpec((1, tm, d), tile),
                  pl.BlockSpec((1, N_MOD, d), lambda i, t: (i, 0, 0)),
                  _resident(mc.shape), _resident((1, d)),
                  pl.BlockSpec((tm, LANES), lambda i, t: (t, 0)),
                  pl.BlockSpec((tm, LANES), lambda i, t: (t, 0))]
                 + [_resident(x.shape) for x in weights],
        out_specs=[pl.BlockSpec((1, tm, n), tile) for n, _ in widths]
                  + [pl.BlockSpec((1, vw, tm), lambda i, t: (i, 0, t))],
        out_shape=[jax.ShapeDtypeStruct((b, s, n), dt) for n, dt in widths]
                  + [jax.ShapeDtypeStruct((b, vw, s), BF16)],
        compiler_params=_params(est, 2),
        name="proj_seq",
    )(h, ml, mc, norm_w.reshape(1, d), cos_t, sin_t, *weights)


def _proj_loc_kernel(h_ref, ml_ref, mc_ref, nw_ref, wpv, wcv, wg, pv_o, cv_o, g_o, *, n_ctx):
    tm = h_ref.shape[1]
    mod = _row_mods(ml_ref, mc_ref, pl.program_id(1), tm, n_ctx)
    a = (_rms(h_ref[0], nw_ref[...]) * (1.0 + mod(4)) + mod(3)).astype(BF16)
    pv_o[0] = _dot(a, wpv[...]).astype(BF16)
    cv_o[0] = _dot(a, wcv[...]).astype(BF16)
    g_o[0] = jax.nn.sigmoid(_dot(a, wg[...])).astype(BF16)


def _proj_loc(h, ml, mc, norm_w, w, *, n_ctx):
    b, s, d = h.shape
    tm = ROW_TILE
    weights = [w["wpv"], w["wcv"], w["wg"]]
    widths = [x.shape[1] for x in weights]
    tile = lambda i, t: (i, t, 0)
    est = (2 * _nbytes((tm, d), F32) + sum(_nbytes(x.shape, BF16) for x in weights)
           + 2 * sum(_nbytes((tm, n), BF16) for n in widths) + 2 * _nbytes((tm, max(widths)), F32))
    return pl.pallas_call(
        functools.partial(_proj_loc_kernel, n_ctx=n_ctx),
        grid=(b, s // tm),
        in_specs=[pl.BlockSpec((1, tm, d), tile),
                  pl.BlockSpec((1, N_MOD, d), lambda i, t: (i, 0, 0)),
                  _resident(mc.shape), _resident((1, d))]
                 + [_resident(x.shape) for x in weights],
        out_specs=[pl.BlockSpec((1, tm, n), tile) for n in widths],
        out_shape=[jax.ShapeDtypeStruct((b, s, n), BF16) for n in widths],
        compiler_params=_params(est, 2),
        name="proj_loc",
    )(h, ml, mc, norm_w.reshape(1, d), *weights)


def _ssd_conv_kernel(x_ref, xp_ref, xn_ref, cw_ref, cb_ref, xs_o, c_o, bt_o, *, n_ctx):
    tm = x_ref.shape[1]
    q = SSD_CHUNK
    n = SSD_STATE
    t = pl.program_id(1)
    seq_start, seq_end = _seq_edges(t, tm, n_ctx, pl.num_programs(1) * tm)
    pad_l = SSD_CONV // 2
    ext_rows = tm + 2 * HALO
    for j in range(SSD_XBC // LANES):
        lc = slice(j * LANES, (j + 1) * LANES)
        ext = jnp.concatenate([jnp.where(seq_start, 0.0, xp_ref[0, :, lc].astype(F32)),
                               x_ref[0, :, lc].astype(F32),
                               jnp.where(seq_end, 0.0, xn_ref[0, :, lc].astype(F32))], axis=0)
        acc = cb_ref[:, lc]
        for k in range(SSD_CONV):
            tap = ext if k == pad_l else pltpu.roll(ext, (pad_l - k) % ext_rows, axis=0)
            acc = acc + cw_ref[k:k + 1, lc] * tap[HALO:HALO + tm]
        act = _silu(acc)
        c0 = j * LANES
        if c0 < SSD_INNER:
            xs_o[0, :, lc] = act.astype(BF16)
        elif c0 < SSD_INNER + SSD_BC:
            g = (c0 - SSD_INNER) // n
            for ci in range(tm // q):
                r0 = (ci * SSD_GROUPS + g) * n
                bt_o[0, r0:r0 + n, :] = act[ci * q:(ci + 1) * q, :].T.astype(BF16)
        else:
            cc = c0 - SSD_INNER - SSD_BC
            c_o[0, :, cc:cc + LANES] = act.astype(BF16)


def _ssd_conv(xbc, conv_w, conv_b, *, n_ctx):
    b, s, _ = xbc.shape
    tm = MIX_TILE
    q = SSD_CHUNK
    assert s % tm == 0 and n_ctx % tm == 0 and tm % q == 0 and SSD_STATE == LANES
    tile = lambda i, t: (i, t, 0)
    prev, nxt = _halo_maps(tm, s)
    bt_rows = tm // q * SSD_BC
    est = (2 * _nbytes((tm + 2 * HALO, SSD_XBC), BF16) + _nbytes((tm + 2 * HALO, SSD_XBC), F32)
           + 2 * _nbytes((tm, SSD_XBC), BF16))
    return pl.pallas_call(
        functools.partial(_ssd_conv_kernel, n_ctx=n_ctx),
        grid=(b, s // tm),
        in_specs=[pl.BlockSpec((1, tm, SSD_XBC), tile),
                  pl.BlockSpec((1, HALO, SSD_XBC), prev),
                  pl.BlockSpec((1, HALO, SSD_XBC), nxt),
                  _resident(conv_w.shape), _resident((1, SSD_XBC))],
        out_specs=[pl.BlockSpec((1, tm, SSD_INNER), tile),
                   pl.BlockSpec((1, tm, SSD_BC), tile),
                   pl.BlockSpec((1, bt_rows, q), tile)],
        out_shape=[jax.ShapeDtypeStruct((b, s, SSD_INNER), BF16),
                   jax.ShapeDtypeStruct((b, s, SSD_BC), BF16),
                   jax.ShapeDtypeStruct((b, s // q * SSD_BC, q), BF16)],
        compiler_params=_params(est, 2),
        name="ssd_conv",
    )(xbc, xbc, xbc, conv_w, conv_b.reshape(1, -1))


def _split_bf16(x, parts):
    out = []
    for _ in range(parts):
        p = x.astype(BF16)
        out.append(p)
        x = x - p.astype(F32)
    return out


class _SsdChunk:
    def __init__(self, d, sub, xs_ref, c_ref, bt_ref, dt_ref, y_ref, st_ref, bias_ref, alog_ref, dskip_ref,
                 expand_ref, tri_ref):
        self.d = d
        self.rows = slice(sub * SSD_CHUNK, (sub + 1) * SSD_CHUNK)
        self.bt_row0 = sub * SSD_BC
        self.xs_ref, self.c_ref, self.bt_ref, self.dt_ref = xs_ref, c_ref, bt_ref, dt_ref
        self.y_ref, self.st_ref = y_ref, st_ref
        self.bias_ref, self.alog_ref, self.dskip_ref = bias_ref, alog_ref, dskip_ref
        self.expand_ref, self.tri_ref = expand_ref, tri_ref

    def cumsum(self):
        x = self.dt_ref[0, self.rows, :] + self.bias_ref[...]
        self.dtv = jnp.maximum(x, 0.0) + jnp.log1p(jnp.exp(-jnp.abs(x)))
        da = self.dtv * (-LOG2E * jnp.exp(self.alog_ref[...]))
        tri = self.tri_ref[self.d]
        self.cs = sum(_dot(tri, p) for p in _split_bf16(da, 3))

    def expand(self):
        q = SSD_CHUNK
        last = q - 1 if self.d == 0 else 0
        cs, dtv = self.cs, self.dtv
        tot = cs[last:last + 1, :]
        self.src_t = (cs - jnp.log2(dtv)).T
        stack = jnp.concatenate(
            [dtv * jnp.exp2(tot - cs), jnp.exp2(cs), jnp.broadcast_to(jnp.exp2(tot), (BF16_ROWS, LANES))], axis=0)
        ex = _dot(stack.astype(BF16), self.expand_ref[self.d])
        self.ecs_exp = ex[q:2 * q]
        self.etot_exp = ex[2 * q:2 * q + 1]
        self.xs = self.xs_ref[0, self.rows, :]
        self.xe = (self.xs.astype(F32) * ex[:q]).astype(BF16)

    def group(self, g):
        d, q, n = self.d, SSD_CHUNK, SSD_STATE
        gw = HEADS_PER_GROUP * SSD_HEAD_DIM
        li = lax.broadcasted_iota(jnp.int32, (q, q), 0)
        si = lax.broadcasted_iota(jnp.int32, (q, q), 1)
        causal = (li >= si) if d == 0 else (li <= si)
        head_of_lane = lax.broadcasted_iota(jnp.int32, (q, gw), 1) // SSD_HEAD_DIM
        gc = slice(g * gw, (g + 1) * gw)
        c_g = self.c_ref[0, self.rows, g * n:(g + 1) * n]
        bt_g = self.bt_ref[0, self.bt_row0 + g * n:self.bt_row0 + (g + 1) * n, :]
        cb = _dot(c_g, bt_g)
        st_g = self.st_ref[:, gc]
        y_off = _dot(c_g, st_g.astype(BF16))
        xs_g = self.xs[:, gc]
        m_parts, x_parts = [], []
        for r in range(HEADS_PER_GROUP):
            col = d * SSD_HEADS + g * HEADS_PER_GROUP + r
            seg = self.cs[:, col:col + 1] - self.src_t[col:col + 1, :]
            m_parts.append((cb * jnp.exp2(jnp.where(causal, seg, -jnp.inf))).astype(BF16))
            x_parts.append(jnp.where(head_of_lane == r, xs_g, jnp.zeros_like(xs_g)))
        y_g = _dot(jnp.concatenate(m_parts, axis=1), jnp.concatenate(x_parts, axis=0))
        y_g = y_g + y_off * self.ecs_exp[:, gc]
        if d == 0:
            y_g = y_g + xs_g.astype(F32) * self.dskip_ref[:, gc]
        self.y_ref[0, self.rows, gc] = y_g.astype(BF16)
        self.st_ref[:, gc] = st_g * self.etot_exp[:, gc] + _dot(bt_g, self.xe[:, gc])


SCAN_CHUNKS = 2


def _ssd_scan_kernel(xs_f, c_f, bt_f, dt_f, xs_b, c_b, bt_b, dt_b, bias_ref, alog_ref, dskip_ref, expand_ref,
                     tri_ref, yf_ref, yb_ref, stf_s, stb_s):
    @pl.when(pl.program_id(1) == 0)
    def _reset():
        stf_s[...] = jnp.zeros(stf_s.shape, F32)
        stb_s[...] = jnp.zeros(stb_s.shape, F32)
    shared = (bias_ref, alog_ref, dskip_ref, expand_ref, tri_ref)
    waves = [(_SsdChunk(0, k, xs_f, c_f, bt_f, dt_f, yf_ref, stf_s, *shared),
              _SsdChunk(1, SCAN_CHUNKS - 1 - k, xs_b, c_b, bt_b, dt_b, yb_ref, stb_s, *shared))
             for k in range(SCAN_CHUNKS)]
    for wave in waves:
        for ch in wave:
            ch.cumsum()
    for wave in waves:
        for ch in wave:
            ch.expand()
    for wave in waves:
        for g in range(SSD_GROUPS):
            for ch in wave:
                ch.group(g)


def _ssd_scan(xs, cm, bt, dt, dt_bias, a_log, d_skip, *, n_ctx):
    b, s, _ = xs.shape
    q = SSD_CHUNK
    rows = SCAN_CHUNKS * q
    assert s % rows == 0 and n_ctx % rows == 0
    nb = s // rows
    nbt = n_ctx // rows
    pad = LANES - 2 * SSD_HEADS
    bias_row = jnp.pad(dt_bias.reshape(1, -1), ((0, 0), (0, pad)))
    alog_row = jnp.pad(a_log.reshape(1, -1), ((0, 0), (0, pad)))
    dskip_row = jnp.repeat(d_skip, SSD_HEAD_DIM).reshape(1, SSD_INNER)
    e = np.zeros((2, LANES, SSD_INNER), np.float32)
    for d in range(2):
        for h in range(SSD_HEADS):
            e[d, d * SSD_HEADS + h, h * SSD_HEAD_DIM:(h + 1) * SSD_HEAD_DIM] = 1.0
    low = np.tril(np.ones((q, q), np.float32))
    tri = np.stack([low, low.T])
    fwd = lambda i, c: (i, c, 0)
    bwd = lambda i, c: (i, jnp.where(c < nbt, nbt - 1 - c, nb - 1 - c + nbt), 0)
    per_dir = lambda m: [pl.BlockSpec((1, rows, SSD_INNER), m), pl.BlockSpec((1, rows, SSD_BC), m),
                         pl.BlockSpec((1, SCAN_CHUNKS * SSD_BC, q), m), pl.BlockSpec((1, rows, LANES), m)]
    est = (4 * (_nbytes((rows, SSD_INNER + SSD_BC), BF16) + _nbytes((SCAN_CHUNKS * SSD_BC, q), BF16)
                + _nbytes((rows, LANES), F32))
           + 4 * _nbytes((rows, SSD_INNER), BF16) + 2 * _nbytes((SSD_STATE, SSD_INNER), F32)
           + _nbytes(e.shape, BF16) + 12 * SCAN_CHUNKS * _nbytes((q, SSD_INNER), F32))
    return pl.pallas_call(
        _ssd_scan_kernel,
        grid=(b, nb),
        in_specs=per_dir(fwd) + per_dir(bwd)
                 + [_resident((1, LANES)), _resident((1, LANES)), _resident((1, SSD_INNER)),
                    _resident(e.shape), _resident(tri.shape)],
        out_specs=[pl.BlockSpec((1, rows, SSD_INNER), fwd), pl.BlockSpec((1, rows, SSD_INNER), bwd)],
        out_shape=[jax.ShapeDtypeStruct((b, s, SSD_INNER), BF16)] * 2,
        scratch_shapes=[pltpu.VMEM((SSD_STATE, SSD_INNER), F32)] * 2,
        compiler_params=_params(est, 2),
        name="ssd_scan",
    )(xs, cm, bt, dt, xs, cm, bt, dt, bias_row, alog_row, dskip_row, jnp.asarray(e, BF16), jnp.asarray(tri, BF16))


def _attn_kernel(*refs, q_blocks):
    qn_refs, qp_refs = refs[:q_blocks], refs[q_blocks:2 * q_blocks]
    kn_ref, kp_ref, vt_ref, o_ref, st_scr, p_scr = refs[2 * q_blocks:]
    kp = kp_ref[0]
    n_keys, n_q = p_scr.shape
    kb = ATTN_KEY_BLOCK
    sub = 8

    def scores(h):
        hc = slice(h * LANES, (h + 1) * LANES)
        qc = jnp.concatenate(
            [jnp.concatenate([qn[0, :, hc], qp[0, :, hc]], axis=1) for qn, qp in zip(qn_refs, qp_refs)], axis=0)
        kc = jnp.concatenate([kn_ref[0, :, hc], kp], axis=1)
        st_scr[h % 2] = _dot_nt(kc, qc)

    scores(0)
    for h in range(MLA_HEADS):
        if h + 1 < MLA_HEADS:
            scores(h + 1)
        st = st_scr.at[h % 2]
        hc = slice(h * LANES, (h + 1) * LANES)
        blocks = [slice(c * kb, (c + 1) * kb) for c in range(n_keys // kb)]
        m8 = None
        for rows in blocks:
            bm = jnp.max(st[rows, :].reshape(kb // sub, sub, n_q), axis=0)
            m8 = bm if m8 is None else jnp.maximum(m8, bm)
        m = jnp.max(m8, axis=0, keepdims=True)
        l8 = jnp.zeros((sub, n_q), F32)
        for rows in blocks:
            p = jnp.exp2(st[rows, :] - m)
            l8 = l8 + jnp.sum(p.reshape(kb // sub, sub, n_q), axis=0)
            p_scr[rows, :] = p.astype(BF16)
        denom = jnp.sum(l8, axis=0, keepdims=True)
        ot = _dot(vt_ref[0, hc, :], p_scr[...])
        o_ref[0, :, hc] = (ot / denom).T.astype(BF16)


def _attention(qn, qp, kn, kp, vt, *, n_ctx):
    b, s, w = qn.shape
    tb = MIX_TILE
    lat_q_blocks = 2
    tq = tb * lat_q_blocks
    assert n_ctx == tb and (s - n_ctx) % tq == 0 and MLA_V == LANES
    ctx_blocks = n_ctx // tb

    def call(q_blocks, n_keys, q_maps, n_tiles, name):
        rows = q_blocks * tb
        whole = lambda i, t: (i, 0, 0)
        est = (4 * _nbytes((rows, w), BF16) + 2 * (2 * _nbytes((n_keys, w), BF16) + _nbytes((n_keys, LANES), BF16))
               + 2 * _nbytes((rows, w), BF16) + 3 * _nbytes((n_keys, rows), F32) + _nbytes((n_keys, 2 * LANES), BF16))
        return pl.pallas_call(
            functools.partial(_attn_kernel, q_blocks=q_blocks),
            grid=(b, n_tiles),
            in_specs=[pl.BlockSpec((1, tb, w), m) for m in q_maps] * 2
                     + [pl.BlockSpec((1, n_keys, w), whole), pl.BlockSpec((1, n_keys, LANES), whole),
                        pl.BlockSpec((1, w, n_keys), whole)],
            out_specs=pl.BlockSpec((1, rows, w), lambda i, t: (i, t, 0)),
            out_shape=jax.ShapeDtypeStruct((b, n_tiles * rows, w), BF16),
            scratch_shapes=[pltpu.VMEM((2, n_keys, rows), F32), pltpu.VMEM((n_keys, rows), BF16)],
            compiler_params=_params(est, 2),
            name=name,
        )(*([qn] * q_blocks), *([qp] * q_blocks), kn, kp, vt)

    att_c = call(1, n_ctx, [lambda i, t: (i, 0, 0)], 1, "attention_ctx")
    lat_maps = [lambda i, t, j=j: (i, ctx_blocks + lat_q_blocks * t + j, 0) for j in range(lat_q_blocks)]
    att_l = call(lat_q_blocks, s, lat_maps, (s - n_ctx) // tq, "attention")
    return att_c, att_l


def _pool_bands(tile_rows):
    t = np.arange(tile_rows)[:, None]
    j = np.arange(tile_rows + 2 * HALO)[None, :]
    bands = []
    for win in POOL_WINDOWS:
        lo = HALO + t - win // 2
        bands.append(((j >= lo) & (j < lo + win)).astype(np.float32))
    return np.stack(bands)


def _merge_kernel(h_ref, ml_ref, mc_ref, yf_ref, yb_ref, zs_ref, attc_ref, attl_ref,
                  pv_ref, pvp_ref, pvn_ref, cv_ref, cvp_ref, cvn_ref, sg_ref,
                  snw_ref, wso, wmo, band_ref, pw_ref, ps_ref, wpo, cw_ref, wco, wo,
                  o_ref, ext_u, *, n_ctx):
    tm = h_ref.shape[1]
    d = h_ref.shape[2]
    t = pl.program_id(1)
    s_len = pl.num_programs(1) * tm
    mod = _row_mods(ml_ref, mc_ref, t, tm, n_ctx)
    seq_start, seq_end = _seq_edges(t, tm, n_ctx, s_len)

    def gate(k):
        return sg_ref[0, :, k * d:(k + 1) * d].astype(F32)

    mla_b = _dot(jnp.where(t < n_ctx // tm, attc_ref[0], attl_ref[0]), wmo[...])
    merged = gate(1) * mla_b

    pv = pv_ref[0]
    ext_p = jnp.concatenate([jnp.where(seq_start, jnp.zeros_like(pvp_ref[0]), pvp_ref[0]), pv,
                             jnp.where(seq_end, jnp.zeros_like(pvn_ref[0]), pvn_ref[0])], axis=0)
    rows = t * tm + lax.broadcasted_iota(jnp.int32, (tm, 1), 0)
    in_ctx = rows < n_ctx
    pos = jnp.where(in_ctx, rows, rows - n_ctx)
    seq_n = jnp.where(in_ctx, n_ctx, s_len - n_ctx)
    pooled = []
    for gi, win in enumerate(POOL_WINDOWS):
        gc = slice(gi * POOL_GROUP, (gi + 1) * POOL_GROUP)
        left = win // 2
        lo = jnp.clip(pos - left, 0, seq_n)
        hi = jnp.clip(pos - left + win, 0, seq_n)
        p_g = _dot(band_ref[gi], ext_p[:, gc]) / (hi - lo).astype(F32) - pv[:, gc].astype(F32)
        pooled.append(_dot(p_g.astype(BF16), pw_ref[gi]))
    pool_y = jnp.concatenate(pooled, axis=1) * ps_ref[...]
    merged = merged + gate(2) * _dot(pool_y.astype(BF16), wpo[...])

    y = yf_ref[0].astype(F32) + yb_ref[0].astype(F32)
    ssd_in = _rms(y * zs_ref[0].astype(F32), snw_ref[...]).astype(BF16)
    merged = merged + gate(0) * _dot(ssd_in, wso[...])

    def gated_in(ref):
        return ref[0, :, CONV_WIDTH:2 * CONV_WIDTH].astype(F32) * ref[0, :, 2 * CONV_WIDTH:].astype(F32)
    ext_u[0:HALO, :] = jnp.where(seq_start, 0.0, gated_in(cvp_ref))
    ext_u[HALO:HALO + tm, :] = gated_in(cv_ref)
    ext_u[HALO + tm:, :] = jnp.where(seq_end, 0.0, gated_in(cvn_ref))
    conv = cw_ref[0:1, :] * ext_u[HALO - 1:HALO - 1 + tm, :]
    for k in range(1, CONV_K):
        conv = conv + cw_ref[k:k + 1, :] * ext_u[HALO - 1 + k:HALO - 1 + k + tm, :]
    conv_in = (cv_ref[0, :, :CONV_WIDTH].astype(F32) * conv).astype(BF16)
    merged = merged + gate(3) * _dot(conv_in, wco[...])
    o_ref[0] = h_ref[0] + mod(5) * _dot(merged.astype(BF16), wo[...])


def _merge(h, ml, mc, yf, yb, zs, att_c, att_l, pv, cv, sg, w, *, n_ctx):
    b, s, d = h.shape
    tm = MIX_TILE
    assert s % tm == 0 and n_ctx % tm == 0 and tm % HALO == 0
    ctx_tiles = n_ctx // tm
    tile = lambda i, t: (i, t, 0)
    prev, nxt = _halo_maps(tm, s)
    bands = _pool_bands(tm)
    weights = [w["ssd_norm"], w["ssd_w_out"], w["mla_w_out"], jnp.asarray(bands, BF16), w["pool_w"],
               w["pool_scale"], w["pool_w_out"], w["sconv_w"], w["sconv_w_out"], w["w_o"]]
    stream = (2 * _nbytes((tm, d), F32) + _nbytes((tm, 4 * SSD_INNER + POOL_WIDTH + 3 * CONV_WIDTH + N_BRANCH * d), BF16)
              + 2 * _nbytes((HALO, POOL_WIDTH + 3 * CONV_WIDTH), BF16))
    est = (2 * stream + sum(_nbytes(x.shape, x.dtype) for x in weights)
           + _nbytes((tm + 2 * HALO, d), F32) + 10 * _nbytes((tm, d), F32))
    return pl.pallas_call(
        functools.partial(_merge_kernel, n_ctx=n_ctx),
        grid=(b, s // tm),
        in_specs=[pl.BlockSpec((1, tm, d), tile),
                  pl.BlockSpec((1, N_MOD, d), lambda i, t: (i, 0, 0)),
                  _resident(mc.shape),
                  pl.BlockSpec((1, tm, SSD_INNER), tile),
                  pl.BlockSpec((1, tm, SSD_INNER), tile),
                  pl.BlockSpec((1, tm, SSD_INNER), tile),
                  pl.BlockSpec((1, tm, MLA_HEADS * MLA_V), lambda i, t: (i, jnp.minimum(t, ctx_tiles - 1), 0)),
                  pl.BlockSpec((1, tm, MLA_HEADS * MLA_V), lambda i, t: (i, jnp.maximum(t - ctx_tiles, 0), 0)),
                  pl.BlockSpec((1, tm, POOL_WIDTH), tile),
                  pl.BlockSpec((1, HALO, POOL_WIDTH), prev),
                  pl.BlockSpec((1, HALO, POOL_WIDTH), nxt),
                  pl.BlockSpec((1, tm, 3 * CONV_WIDTH), tile),
                  pl.BlockSpec((1, HALO, 3 * CONV_WIDTH), prev),
                  pl.BlockSpec((1, HALO, 3 * CONV_WIDTH), nxt),
                  pl.BlockSpec((1, tm, N_BRANCH * d), tile)]
                 + [_resident(x.shape) for x in weights],
        out_specs=pl.BlockSpec((1, tm, d), tile),
        out_shape=jax.ShapeDtypeStruct(h.shape, F32),
        scratch_shapes=[pltpu.VMEM((tm + 2 * HALO, CONV_WIDTH), F32)],
        compiler_params=_params(est, 2),
        name="merge",
    )(h, ml, mc, yf, yb, zs, att_c, att_l, pv, pv, pv, cv, cv, cv, sg, *weights)


def _rope_tables(n_ctx, n_lat):
    rows = n_lat // GRID_W
    row = np.repeat(np.arange(rows), GRID_W).astype(np.float32)
    col = np.tile(np.arange(GRID_W), rows).astype(np.float32)
    half = MLA_ROPE // 2
    inv = (1.0 / (ROPE_THETA ** (jnp.arange(0, half, 2, dtype=F32) / half)))
    ar = jnp.asarray(row)[:, None] * inv
    ac = jnp.asarray(col)[:, None] * inv
    ang = jnp.concatenate([ar, ar, ac, ac], axis=-1)
    pad = LANES - MLA_ROPE
    cos = jnp.concatenate([jnp.ones((n_ctx, MLA_ROPE), F32), jnp.cos(ang)], axis=0)
    sin = jnp.concatenate([jnp.zeros((n_ctx, MLA_ROPE), F32), jnp.sin(ang)], axis=0)
    return jnp.pad(cos, ((0, 0), (0, pad))), jnp.pad(sin, ((0, 0), (0, pad)))


def _rot_cols(w):
    w1, w2, w3, w4 = jnp.split(w, 4, axis=-1)
    return jnp.concatenate([-w2, w1, -w4, w3], axis=-1)


def _layer_weights(i, p):
    w_in = p["w_in"][i]
    splits = np.cumsum([SSD_INNER, SSD_XBC, 2 * SSD_HEADS, MLA_Q_RANK, MLA_KV_RANK, MLA_ROPE,
                        POOL_WIDTH, 3 * CONV_WIDTH])
    wz, wxbc, wdt, wqd, wkvd, wkr, wpv, wcv, wg = jnp.split(w_in, splits, axis=1)
    lane_pad = lambda x: jnp.pad(x, ((0, 0), (0, LANES - x.shape[1])))
    uq = p["mla_w_uq"][i].reshape(MLA_Q_RANK, MLA_HEADS, MLA_QK)
    uq_pe = uq[:, :, MLA_NOPE:]
    head_pad = lambda x: jnp.pad(x, ((0, 0), (0, 0), (0, LANES - MLA_ROPE))).reshape(MLA_Q_RANK, MLA_HEADS * LANES)
    bf = lambda x: x.astype(BF16)
    return {
        "wz": bf(wz), "wxbc": bf(wxbc),
        "wsm": bf(jnp.concatenate([wqd, wkvd, lane_pad(wkr), lane_pad(_rot_cols(wkr)), lane_pad(wdt)], axis=1)),
        "q_norm": p["mla_q_norm"][i].reshape(1, -1), "kv_norm": p["mla_kv_norm"][i].reshape(1, -1),
        "wuqn": bf(uq[:, :, :MLA_NOPE].reshape(MLA_Q_RANK, MLA_HEADS * MLA_NOPE)),
        "wuqp": bf(head_pad(uq_pe)), "wuqr": bf(head_pad(_rot_cols(uq_pe))),
        "wuk": bf(p["mla_w_uk"][i]), "wuvt": bf(p["mla_w_uv"][i].T),
        "wpv": bf(wpv), "wcv": bf(wcv), "wg": bf(wg),
        "ssd_norm": p["ssd_norm"][i].reshape(1, -1), "ssd_w_out": bf(p["ssd_w_out"][i]),
        "mla_w_out": bf(p["mla_w_out"][i]), "pool_w": bf(p["pool_w"][i]),
        "pool_scale": p["pool_scale"][i].reshape(1, -1), "pool_w_out": bf(p["pool_w_out"][i]),
        "sconv_w": p["sconv_w"][i], "sconv_w_out": bf(p["sconv_w_out"][i]), "w_o": bf(p["w_o"][i]),
    }


def kernel(x, c, ctx, c_ctx, w_ada, b_ada, ffn1_norm, ffn1_w_gate, ffn1_w_up, ffn1_w_down, mix_norm, w_in, ssd_conv_w, ssd_conv_b, ssd_dt_bias, ssd_a_log, ssd_d, ssd_norm, ssd_w_out, mla_q_norm, mla_w_uq, mla_kv_norm, mla_w_uk, mla_w_uv, mla_w_out, pool_w, pool_scale, pool_w_out, sconv_w, sconv_w_out, w_o, ffn2_norm, ffn2_w_gate, ffn2_w_up, ffn2_w_down, final_norm):
    p = dict(w_in=w_in, ssd_norm=ssd_norm, ssd_w_out=ssd_w_out, mla_q_norm=mla_q_norm, mla_w_uq=mla_w_uq,
             mla_kv_norm=mla_kv_norm, mla_w_uk=mla_w_uk, mla_w_uv=mla_w_uv, mla_w_out=mla_w_out,
             pool_w=pool_w, pool_scale=pool_scale, pool_w_out=pool_w_out, sconv_w=sconv_w,
             sconv_w_out=sconv_w_out, w_o=w_o)
    bsz, n_lat, d = x.shape
    n_ctx = ctx.shape[1]
    depth = w_ada.shape[0]

    mod_rows = -(-(bsz + 1) // 8) * 8
    cc = jnp.concatenate([c, c_ctx[None, :], jnp.zeros((mod_rows - bsz - 1, d), F32)], axis=0)
    mods = _adaln(cc, w_ada, b_ada).reshape(depth, mod_rows, N_MOD, d)

    cos_t, sin_t = _rope_tables(n_ctx, n_lat)
    h = x
    for i in range(depth):
        ml, mc = mods[i, :bsz], mods[i, bsz]
        w = _layer_weights(i, p)
        bf = lambda a: a.astype(BF16)
        h = _ffn(h, ml, mc, ffn1_norm[i], bf(ffn1_w_gate[i]), bf(ffn1_w_up[i]), bf(ffn1_w_down[i]),
                 n_ctx=n_ctx, mod0=0, ctx=ctx if i == 0 else None)
        zs, xbc, dt, qn, qp, kn, kp, vt = _proj_seq(h, ml, mc, mix_norm[i], cos_t, sin_t, w, n_ctx=n_ctx)
        pv, cv, sg = _proj_loc(h, ml, mc, mix_norm[i], w, n_ctx=n_ctx)
        xs, cm, bt = _ssd_conv(xbc, ssd_conv_w[i], ssd_conv_b[i], n_ctx=n_ctx)
        yf, yb = _ssd_scan(xs, cm, bt, dt, ssd_dt_bias[i], ssd_a_log[i], ssd_d[i], n_ctx=n_ctx)
        att_c, att_l = _attention(qn, qp, kn, kp, vt, n_ctx=n_ctx)
        h = _merge(h, ml, mc, yf, yb, zs, att_c, att_l, pv, cv, sg, w, n_ctx=n_ctx)
        h = _ffn(h, ml, mc, ffn2_norm[i], bf(ffn2_w_gate[i]), bf(ffn2_w_up[i]), bf(ffn2_w_down[i]),
                 n_ctx=n_ctx, mod0=6, final_w=final_norm if i == depth - 1 else None)
    return h
```

```python
import functools
import math

import jax
import jax.numpy as jnp
import numpy as np
from jax import lax
from jax.experimental import pallas as pl
from jax.experimental.pallas import tpu as pltpu

F32 = jnp.float32
BF16 = jnp.bfloat16

EPS = 1e-6
LOG2E = math.log2(math.e)
FFN_RES = 0.5
N_MOD = 9
GRID_W = 64
ROPE_THETA = 10000.0

SSD_HEADS = 16
SSD_HEAD_DIM = 64
SSD_GROUPS = 4
SSD_STATE = 128
SSD_CONV = 4
SSD_CHUNK = 128
SSD_INNER = SSD_HEADS * SSD_HEAD_DIM
SSD_BC = SSD_GROUPS * SSD_STATE
SSD_XBC = SSD_INNER + 2 * SSD_BC
HEADS_PER_GROUP = SSD_HEADS // SSD_GROUPS

MLA_HEADS = 8
MLA_Q_RANK = 384
MLA_KV_RANK = 256
MLA_NOPE = 128
MLA_ROPE = 64
MLA_V = 128
MLA_QK = MLA_NOPE + MLA_ROPE

POOL_WINDOWS = (2, 4, 8, 16)
POOL_GROUP = 256
POOL_WIDTH = POOL_GROUP * len(POOL_WINDOWS)
CONV_WIDTH = 1024
CONV_K = 3
N_BRANCH = 4

LANES = 128
BF16_ROWS = 16
HALO = BF16_ROWS
VMEM_CAP = 56 * 1024 * 1024
VMEM_SLACK = 12 * 1024 * 1024

ROW_TILE = 768
MIX_TILE = 256
FFN_CHUNK = 256
ATTN_KEY_BLOCK = 256


def _params(est_bytes, n_axes):
    return pltpu.CompilerParams(
        dimension_semantics=("arbitrary",) * n_axes,
        vmem_limit_bytes=int(min(est_bytes + VMEM_SLACK, VMEM_CAP)))


def _resident(shape):
    nd = len(shape)
    return pl.BlockSpec(shape, lambda *_: (0,) * nd, pipeline_mode=pl.Buffered(1))


def _nbytes(shape, dtype):
    return int(np.prod(shape)) * jnp.dtype(dtype).itemsize


def _dot(a, b):
    return jnp.dot(a, b, preferred_element_type=F32)


def _dot_nt(a, b):
    return lax.dot_general(a, b, (((1,), (1,)), ((), ())), preferred_element_type=F32)


def _silu(x):
    return x * jax.nn.sigmoid(x)


def _rms(x, w):
    ms = jnp.mean(x * x, axis=-1, keepdims=True)
    return x * lax.rsqrt(ms + EPS) * w


def _row_mods(ml_ref, mc_ref, tile, tile_rows, n_ctx):
    rows = tile * tile_rows + lax.broadcasted_iota(jnp.int32, (tile_rows, 1), 0)
    is_ctx = rows < n_ctx

    def mod(k):
        return jnp.where(is_ctx, mc_ref[k:k + 1, :], ml_ref[0, k:k + 1, :])
    return mod


def _halo_maps(tile_rows, s_len):
    hb = tile_rows // HALO
    last = s_len // HALO - 1
    prev = lambda i, t: (i, jnp.maximum(t * hb - 1, 0), 0)
    nxt = lambda i, t: (i, jnp.minimum((t + 1) * hb, last), 0)
    return prev, nxt


def _seq_edges(tile, tile_rows, n_ctx, s_len):
    row0 = tile * tile_rows
    start = jnp.logical_or(row0 == 0, row0 == n_ctx)
    end = jnp.logical_or(row0 + tile_rows == n_ctx, row0 + tile_rows == s_len)
    return start, end


def _ada_kernel(c_ref, w_ref, b_ref, o_ref):
    s = _silu(c_ref[...]).astype(BF16)
    o_ref[0] = _dot(s, w_ref[0].astype(BF16)) + b_ref[0]


def _adaln(cc, w_ada, b_ada):
    depth, d, n = w_ada.shape
    rows = cc.shape[0]
    tn = n // 8
    est = 2 * (_nbytes((d, tn), F32) + _nbytes((rows, tn), F32)) + _nbytes((d, tn), BF16)
    return pl.pallas_call(
        _ada_kernel,
        grid=(depth, n // tn),
        in_specs=[pl.BlockSpec((rows, d), lambda i, j: (0, 0)),
                  pl.BlockSpec((1, d, tn), lambda i, j: (i, 0, j)),
                  pl.BlockSpec((1, 1, tn), lambda i, j: (i, 0, j))],
        out_specs=pl.BlockSpec((1, rows, tn), lambda i, j: (i, 0, j)),
        out_shape=jax.ShapeDtypeStruct((depth, rows, n), F32),
        compiler_params=_params(est, 2),
        name="adaln",
    )(cc, w_ada, b_ada.reshape(depth, 1, n))


def _ffn_body(h, mod, mod0, nw_ref, wg_ref, wu_ref, wd_ref, p_scr):
    d_ff = wg_ref.shape[1]
    a = (_rms(h, nw_ref[...]) * (1.0 + mod(mod0 + 1)) + mod(mod0)).astype(BF16)
    for j in range(d_ff // FFN_CHUNK):
        cols = slice(j * FFN_CHUNK, (j + 1) * FFN_CHUNK)
        g = _dot(a, wg_ref[:, cols])
        u = _dot(a, wu_ref[:, cols])
        p_scr[:, cols] = (_silu(g) * u).astype(BF16)
    y = _dot(p_scr[...], wd_ref[...])
    return h + (FFN_RES * mod(mod0 + 2)) * y


def _ffn_kernel(h_ref, ml_ref, mc_ref, nw_ref, wg_ref, wu_ref, wd_ref, o_ref, p_scr, *, n_ctx, mod0):
    tm = h_ref.shape[1]
    mod = _row_mods(ml_ref, mc_ref, pl.program_id(1), tm, n_ctx)
    o_ref[0] = _ffn_body(h_ref[0], mod, mod0, nw_ref, wg_ref, wu_ref, wd_ref, p_scr)


def _ffn_first_kernel(ctx_ref, x_ref, ml_ref, mc_ref, nw_ref, wg_ref, wu_ref, wd_ref, o_ref, p_scr, *, n_ctx, mod0):
    tm = x_ref.shape[1]
    t = pl.program_id(1)
    mod = _row_mods(ml_ref, mc_ref, t, tm, n_ctx)
    h = jnp.where(t < n_ctx // tm, ctx_ref[0], x_ref[0])
    o_ref[0] = _ffn_body(h, mod, mod0, nw_ref, wg_ref, wu_ref, wd_ref, p_scr)


def _ffn_last_kernel(h_ref, ml_ref, mc_ref, nw_ref, wg_ref, wu_ref, wd_ref, fw_ref, o_ref, p_scr, *, n_ctx, mod0):
    tm = h_ref.shape[1]
    mod = _row_mods(ml_ref, mc_ref, pl.program_id(1), tm, n_ctx)
    o_ref[0] = _rms(_ffn_body(h_ref[0], mod, mod0, nw_ref, wg_ref, wu_ref, wd_ref, p_scr), fw_ref[...])


def _ffn(h, ml, mc, norm_w, wg, wu, wd, layer, *, n_ctx, mod0, ctx=None, final_w=None):
    b, _, d = h.shape
    d_ff = wg.shape[2]
    layer_block = lambda x: pl.BlockSpec((None,) + x.shape[1:], lambda *_: (layer, 0, 0),
                                         pipeline_mode=pl.Buffered(1))
    special = ctx is not None or final_w is not None
    tm = MIX_TILE if special else ROW_TILE
    s = h.shape[1] + (ctx.shape[1] if ctx is not None else 0)
    assert s % tm == 0 and d_ff % FFN_CHUNK == 0 and (not special or n_ctx % tm == 0)
    ctx_tiles = n_ctx // tm
    est = (4 * _nbytes((tm, d), F32) + 3 * _nbytes((d, d_ff), BF16) + _nbytes((tm, d_ff), BF16)
           + 4 * _nbytes((tm, FFN_CHUNK), F32) + 2 * _nbytes((tm, d), F32))
    tile = lambda i, t: (i, t, 0)
    lat_tile = lambda i, t: (i, jnp.maximum(t - ctx_tiles, 0), 0)
    common = [pl.BlockSpec((1, N_MOD, d), lambda i, t: (i, 0, 0)),
              _resident(mc.shape), _resident((1, d)),
              layer_block(wg), layer_block(wu), layer_block(wd)]
    args = [ml, mc, norm_w.reshape(1, d), wg, wu, wd]
    out_rows, out_spec = s, pl.BlockSpec((1, tm, d), tile)
    if ctx is not None:
        body = _ffn_first_kernel
        in_specs = [pl.BlockSpec((1, tm, d), lambda i, t: (i, jnp.minimum(t, ctx_tiles - 1), 0)),
                    pl.BlockSpec((1, tm, d), lat_tile)] + common
        args = [ctx, h] + args
    elif final_w is not None:
        body = _ffn_last_kernel
        in_specs = [pl.BlockSpec((1, tm, d), tile)] + common + [_resident((1, d))]
        args = [h] + args + [final_w.reshape(1, d)]
        out_rows, out_spec = s - n_ctx, pl.BlockSpec((1, tm, d), lat_tile)
    else:
        body = _ffn_kernel
        in_specs = [pl.BlockSpec((1, tm, d), tile)] + common
        args = [h] + args
    return pl.pallas_call(
        functools.partial(body, n_ctx=n_ctx, mod0=mod0),
        grid=(b, s // tm),
        in_specs=in_specs,
        out_specs=out_spec,
        out_shape=jax.ShapeDtypeStruct((b, out_rows, d), F32),
        scratch_shapes=[pltpu.VMEM((tm, d_ff), BF16)],
        compiler_params=_params(est, 2),
        name="ffn",
    )(*args)


def _proj_seq_kernel(h_ref, hp_ref, hn_ref, ml_ref, mc_ref, nw_ref, cos_ref, sin_ref, cw_ref, cb_ref,
                     wz, wxbc, wsm, qnw_ref, kvnw_ref, wuqn, wuqp, wuqr, wuk, wuvt,
                     zs_o, xs_o, c_o, bt_o, dt_o, qn_o, qp_o, kn_o, kp_o, vt_o, *, n_ctx, qk_scale):
    tm = h_ref.shape[1]
    t = pl.program_id(1)
    s_len = pl.num_programs(1) * tm
    row0 = t * tm

    def normed(ref, first_row):
        rows = first_row + lax.broadcasted_iota(jnp.int32, (ref.shape[1], 1), 0)
        is_ctx = rows < n_ctx
        mod = lambda k: jnp.where(is_ctx, mc_ref[k:k + 1, :], ml_ref[0, k:k + 1, :])
        return (_rms(ref[0], nw_ref[...]) * (1.0 + mod(4)) + mod(3)).astype(BF16)

    a = normed(h_ref, row0)
    a_ext = jnp.concatenate([normed(hp_ref, row0 - HALO), a, normed(hn_ref, row0 + tm)], axis=0)
    xbc = _dot(a_ext, wxbc[...])
    zs_o[0] = _silu(_dot(a, wz[...])).astype(BF16)
    sm = _dot(a, wsm[...])
    c0 = MLA_Q_RANK + MLA_KV_RANK
    qd, kvd, kr, kr_rot = sm[:, :MLA_Q_RANK], sm[:, MLA_Q_RANK:c0], sm[:, c0:c0 + LANES], sm[:, c0 + LANES:c0 + 2 * LANES]
    dt_o[0] = sm[:, c0 + 2 * LANES:]
    cos = cos_ref[...]
    sin = sin_ref[...]
    cos_h = jnp.concatenate([cos] * MLA_HEADS, axis=1)
    sin_h = jnp.concatenate([sin] * MLA_HEADS, axis=1)
    qn = _rms(qd, qnw_ref[...]).astype(BF16)
    qn_o[0] = (_dot(qn, wuqn[...]) * qk_scale).astype(BF16)
    qp_o[0] = ((_dot(qn, wuqp[...]) * cos_h + _dot(qn, wuqr[...]) * sin_h) * qk_scale).astype(BF16)
    ckv = _rms(kvd, kvnw_ref[...]).astype(BF16)
    kn_o[0] = _dot(ckv, wuk[...]).astype(BF16)
    vt_o[0] = _dot_nt(wuvt[...], ckv).astype(BF16)
    kp_o[0] = (kr * cos + kr_rot * sin).astype(BF16)

    ext_rows = tm + 2 * HALO
    pad_l = SSD_CONV // 2
    rows = row0 + lax.broadcasted_iota(jnp.int32, (tm, LANES), 0)
    in_ctx = rows < n_ctx
    pos = jnp.where(in_ctx, rows, rows - n_ctx)
    seq_n = jnp.where(in_ctx, n_ctx, s_len - n_ctx)
    valid = [pos + (k - pad_l) >= 0 if k < pad_l else pos + (k - pad_l) < seq_n for k in range(SSD_CONV)]
    sub = 8
    starts = sorted({0, n_ctx % tm // sub * sub})
    ends = sorted({(n_ctx - 1) % tm // sub * sub, (tm - 1) // sub * sub})

    def masked(tap, k):
        pieces, r = [], 0
        for g in (starts if k < pad_l else ends):
            pieces += [tap[r:g], jnp.where(valid[k][g:g + sub], tap[g:g + sub], 0.0)]
            r = g + sub
        return jnp.concatenate([p for p in pieces + [tap[r:]] if p.shape[0]], axis=0)

    q, n = SSD_CHUNK, SSD_STATE
    for j in range(SSD_XBC // LANES):
        lc = slice(j * LANES, (j + 1) * LANES)
        ext = xbc[:, lc]
        acc = cb_ref[:, lc] + cw_ref[pad_l:pad_l + 1, lc] * ext[HALO:HALO + tm]
        for k in range(SSD_CONV):
            if k != pad_l:
                tap = pltpu.roll(ext, (pad_l - k) % ext_rows, axis=0)[HALO:HALO + tm]
                acc = acc + cw_ref[k:k + 1, lc] * masked(tap, k)
        act = _silu(acc)
        col = j * LANES
        if col < SSD_INNER:
            xs_o[0, :, lc] = act.astype(BF16)
        elif col < SSD_INNER + SSD_BC:
            g = (col - SSD_INNER) // n
            for ci in range(tm // q):
                r0 = (ci * SSD_GROUPS + g) * n
                bt_o[0, r0:r0 + n, :] = act[ci * q:(ci + 1) * q, :].T.astype(BF16)
        else:
            cc = col - SSD_INNER - SSD_BC
            c_o[0, :, cc:cc + LANES] = act.astype(BF16)


def _proj_seq(h, ml, mc, norm_w, cos_t, sin_t, conv_w, conv_b, w, *, n_ctx):
    b, s, d = h.shape
    tm = ROW_TILE
    q = SSD_CHUNK
    assert s % tm == 0 and tm % q == 0 and tm % HALO == 0 and n_ctx % HALO == 0 and SSD_STATE == LANES
    assert MLA_Q_RANK % LANES == 0 and MLA_KV_RANK % LANES == 0
    weights = [w["wz"], w["wxbc"], w["wsm"], w["q_norm"], w["kv_norm"],
               w["wuqn"], w["wuqp"], w["wuqr"], w["wuk"], w["wuvt"]]
    widths = [(SSD_INNER, BF16), (SSD_INNER, BF16), (SSD_BC, BF16), None, (LANES, F32),
              (MLA_HEADS * MLA_NOPE, BF16), (MLA_HEADS * LANES, BF16), (MLA_HEADS * MLA_NOPE, BF16), (LANES, BF16)]
    tile = lambda i, t: (i, t, 0)
    prev, nxt = _halo_maps(tm, s)
    vw = MLA_HEADS * MLA_V
    bt_rows = tm // q * SSD_BC
    row_outs = [x for x in widths if x is not None]
    est = (2 * _nbytes((tm + 2 * HALO, d), F32) + sum(_nbytes(x.shape, x.dtype) for x in weights)
           + 2 * sum(_nbytes((tm, n), dt) for n, dt in row_outs) + 2 * _nbytes((vw, tm), BF16)
           + 2 * _nbytes((bt_rows, q), BF16) + 4 * _nbytes((tm + 2 * HALO, SSD_XBC), F32))
    out_specs, out_shape = [], []
    for x in widths:
        if x is None:
            out_specs.append(pl.BlockSpec((1, bt_rows, q), tile))
            out_shape.append(jax.ShapeDtypeStruct((b, s // q * SSD_BC, q), BF16))
        else:
            out_specs.append(pl.BlockSpec((1, tm, x[0]), tile))
            out_shape.append(jax.ShapeDtypeStruct((b, s, x[0]), x[1]))
    return pl.pallas_call(
        functools.partial(_proj_seq_kernel, n_ctx=n_ctx, qk_scale=MLA_QK ** -0.5 * LOG2E),
        grid=(b, s // tm),
        in_specs=[pl.BlockSpec((1, tm, d), tile),
                  pl.BlockSpec((1, HALO, d), prev),
                  pl.BlockSpec((1, HALO, d), nxt),
                  pl.BlockSpec((1, N_MOD, d), lambda i, t: (i, 0, 0)),
                  _resident(mc.shape), _resident((1, d)),
                  pl.BlockSpec((tm, LANES), lambda i, t: (t, 0)),
                  pl.BlockSpec((tm, LANES), lambda i, t: (t, 0)),
                  _resident(conv_w.shape), _resident((1, SSD_XBC))]
                 + [_resident(x.shape) for x in weights],
        out_specs=out_specs + [pl.BlockSpec((1, vw, tm), lambda i, t: (i, 0, t))],
        out_shape=out_shape + [jax.ShapeDtypeStruct((b, vw, s), BF16)],
        compiler_params=_params(est, 2),
        name="proj_seq",
    )(h, h, h, ml, mc, norm_w.reshape(1, d), cos_t, sin_t, conv_w, conv_b.reshape(1, -1), *weights)


def _proj_loc_kernel(h_ref, ml_ref, mc_ref, nw_ref, wpv, wcv, wg, pv_o, cv_o, g_o, *, n_ctx):
    tm = h_ref.shape[1]
    mod = _row_mods(ml_ref, mc_ref, pl.program_id(1), tm, n_ctx)
    a = (_rms(h_ref[0], nw_ref[...]) * (1.0 + mod(4)) + mod(3)).astype(BF16)
    pv_o[0] = _dot(a, wpv[...]).astype(BF16)
    cv_o[0] = _dot(a, wcv[...]).astype(BF16)
    g_o[0] = jax.nn.sigmoid(_dot(a, wg[...])).astype(BF16)


def _proj_loc(h, ml, mc, norm_w, w, *, n_ctx):
    b, s, d = h.shape
    tm = ROW_TILE
    weights = [w["wpv"], w["wcv"], w["wg"]]
    widths = [x.shape[1] for x in weights]
    tile = lambda i, t: (i, t, 0)
    est = (2 * _nbytes((tm, d), F32) + sum(_nbytes(x.shape, BF16) for x in weights)
           + 2 * sum(_nbytes((tm, n), BF16) for n in widths) + 2 * _nbytes((tm, max(widths)), F32))
    return pl.pallas_call(
        functools.partial(_proj_loc_kernel, n_ctx=n_ctx),
        grid=(b, s // tm),
        in_specs=[pl.BlockSpec((1, tm, d), tile),
                  pl.BlockSpec((1, N_MOD, d), lambda i, t: (i, 0, 0)),
                  _resident(mc.shape), _resident((1, d))]
                 + [_resident(x.shape) for x in weights],
        out_specs=[pl.BlockSpec((1, tm, n), tile) for n in widths],
        out_shape=[jax.ShapeDtypeStruct((b, s, n), BF16) for n in widths],
        compiler_params=_params(est, 2),
        name="proj_loc",
    )(h, ml, mc, norm_w.reshape(1, d), *weights)


def _split_bf16(x, parts):
    out = []
    for _ in range(parts):
        p = x.astype(BF16)
        out.append(p)
        x = x - p.astype(F32)
    return out


class _SsdChunk:
    def __init__(self, d, sub, xs_ref, c_ref, bt_ref, dt_ref, y_ref, st_ref, bias_ref, alog_ref, dskip_ref,
                 expand_ref, tri_ref):
        self.d = d
        self.rows = slice(sub * SSD_CHUNK, (sub + 1) * SSD_CHUNK)
        self.bt_row0 = sub * SSD_BC
        self.xs_ref, self.c_ref, self.bt_ref, self.dt_ref = xs_ref, c_ref, bt_ref, dt_ref
        self.y_ref, self.st_ref = y_ref, st_ref
        self.bias_ref, self.alog_ref, self.dskip_ref = bias_ref, alog_ref, dskip_ref
        self.expand_ref, self.tri_ref = expand_ref, tri_ref

    def cumsum(self):
        x = self.dt_ref[0, self.rows, :] + self.bias_ref[...]
        self.dtv = jnp.maximum(x, 0.0) + jnp.log1p(jnp.exp(-jnp.abs(x)))
        da = self.dtv * (-LOG2E * jnp.exp(self.alog_ref[...]))
        tri = self.tri_ref[self.d]
        self.cs = sum(_dot(tri, p) for p in _split_bf16(da, 3))

    def expand(self):
        q = SSD_CHUNK
        last = q - 1 if self.d == 0 else 0
        cs, dtv = self.cs, self.dtv
        tot = cs[last:last + 1, :]
        self.src_t = (cs - jnp.log2(dtv)).T
        stack = jnp.concatenate(
            [dtv * jnp.exp2(tot - cs), jnp.exp2(cs), jnp.broadcast_to(jnp.exp2(tot), (BF16_ROWS, LANES))], axis=0)
        ex = _dot(stack.astype(BF16), self.expand_ref[self.d])
        self.ecs_exp = ex[q:2 * q]
        self.etot_exp = ex[2 * q:2 * q + 1]
        self.xs = self.xs_ref[0, self.rows, :]
        self.xe = (self.xs.astype(F32) * ex[:q]).astype(BF16)

    def group(self, g):
        d, q, n = self.d, SSD_CHUNK, SSD_STATE
        gw = HEADS_PER_GROUP * SSD_HEAD_DIM
        li = lax.broadcasted_iota(jnp.int32, (q, q), 0)
        si = lax.broadcasted_iota(jnp.int32, (q, q), 1)
        causal = (li >= si) if d == 0 else (li <= si)
        head_of_lane = lax.broadcasted_iota(jnp.int32, (q, gw), 1) // SSD_HEAD_DIM
        gc = slice(g * gw, (g + 1) * gw)
        c_g = self.c_ref[0, self.rows, g * n:(g + 1) * n]
        bt_g = self.bt_ref[0, self.bt_row0 + g * n:self.bt_row0 + (g + 1) * n, :]
        cb = _dot(c_g, bt_g)
        st_g = self.st_ref[:, gc]
        y_off = _dot(c_g, st_g.astype(BF16))
        xs_g = self.xs[:, gc]
        m_parts, x_parts = [], []
        for r in range(HEADS_PER_GROUP):
            col = d * SSD_HEADS + g * HEADS_PER_GROUP + r
            seg = self.cs[:, col:col + 1] - self.src_t[col:col + 1, :]
            m_parts.append((cb * jnp.exp2(jnp.where(causal, seg, -jnp.inf))).astype(BF16))
            x_parts.append(jnp.where(head_of_lane == r, xs_g, jnp.zeros_like(xs_g)))
        y_g = _dot(jnp.concatenate(m_parts, axis=1), jnp.concatenate(x_parts, axis=0))
        y_g = y_g + y_off * self.ecs_exp[:, gc]
        if d == 0:
            y_g = y_g + xs_g.astype(F32) * self.dskip_ref[:, gc]
        self.y_ref[0, self.rows, gc] = y_g.astype(BF16)
        self.st_ref[:, gc] = st_g * self.etot_exp[:, gc] + _dot(bt_g, self.xe[:, gc])


SCAN_CHUNKS = 2


def _ssd_scan_kernel(xs_f, c_f, bt_f, dt_f, xs_b, c_b, bt_b, dt_b, bias_ref, alog_ref, dskip_ref, expand_ref,
                     tri_ref, yf_ref, yb_ref, stf_s, stb_s):
    @pl.when(pl.program_id(1) == 0)
    def _reset():
        stf_s[...] = jnp.zeros(stf_s.shape, F32)
        stb_s[...] = jnp.zeros(stb_s.shape, F32)
    shared = (bias_ref, alog_ref, dskip_ref, expand_ref, tri_ref)
    waves = [(_SsdChunk(0, k, xs_f, c_f, bt_f, dt_f, yf_ref, stf_s, *shared),
              _SsdChunk(1, SCAN_CHUNKS - 1 - k, xs_b, c_b, bt_b, dt_b, yb_ref, stb_s, *shared))
             for k in range(SCAN_CHUNKS)]
    for wave in waves:
        for ch in wave:
            ch.cumsum()
    for wave in waves:
        for ch in wave:
            ch.expand()
    for wave in waves:
        for g in range(SSD_GROUPS):
            for ch in wave:
                ch.group(g)


def _ssd_scan(xs, cm, bt, dt, dt_bias, a_log, d_skip, *, n_ctx):
    b, s, _ = xs.shape
    q = SSD_CHUNK
    rows = SCAN_CHUNKS * q
    assert s % rows == 0 and n_ctx % rows == 0
    nb = s // rows
    nbt = n_ctx // rows
    pad = LANES - 2 * SSD_HEADS
    bias_row = jnp.pad(dt_bias.reshape(1, -1), ((0, 0), (0, pad)))
    alog_row = jnp.pad(a_log.reshape(1, -1), ((0, 0), (0, pad)))
    dskip_row = jnp.repeat(d_skip, SSD_HEAD_DIM).reshape(1, SSD_INNER)
    e = np.zeros((2, LANES, SSD_INNER), np.float32)
    for d in range(2):
        for h in range(SSD_HEADS):
            e[d, d * SSD_HEADS + h, h * SSD_HEAD_DIM:(h + 1) * SSD_HEAD_DIM] = 1.0
    low = np.tril(np.ones((q, q), np.float32))
    tri = np.stack([low, low.T])
    fwd = lambda i, c: (i, c, 0)
    bwd = lambda i, c: (i, jnp.where(c < nbt, nbt - 1 - c, nb - 1 - c + nbt), 0)
    per_dir = lambda m: [pl.BlockSpec((1, rows, SSD_INNER), m), pl.BlockSpec((1, rows, SSD_BC), m),
                         pl.BlockSpec((1, SCAN_CHUNKS * SSD_BC, q), m), pl.BlockSpec((1, rows, LANES), m)]
    est = (4 * (_nbytes((rows, SSD_INNER + SSD_BC), BF16) + _nbytes((SCAN_CHUNKS * SSD_BC, q), BF16)
                + _nbytes((rows, LANES), F32))
           + 4 * _nbytes((rows, SSD_INNER), BF16) + 2 * _nbytes((SSD_STATE, SSD_INNER), F32)
           + _nbytes(e.shape, BF16) + 12 * SCAN_CHUNKS * _nbytes((q, SSD_INNER), F32))
    return pl.pallas_call(
        _ssd_scan_kernel,
        grid=(b, nb),
        in_specs=per_dir(fwd) + per_dir(bwd)
                 + [_resident((1, LANES)), _resident((1, LANES)), _resident((1, SSD_INNER)),
                    _resident(e.shape), _resident(tri.shape)],
        out_specs=[pl.BlockSpec((1, rows, SSD_INNER), fwd), pl.BlockSpec((1, rows, SSD_INNER), bwd)],
        out_shape=[jax.ShapeDtypeStruct((b, s, SSD_INNER), BF16)] * 2,
        scratch_shapes=[pltpu.VMEM((SSD_STATE, SSD_INNER), F32)] * 2,
        compiler_params=_params(est, 2),
        name="ssd_scan",
    )(xs, cm, bt, dt, xs, cm, bt, dt, bias_row, alog_row, dskip_row, jnp.asarray(e, BF16), jnp.asarray(tri, BF16))


def _attn_kernel(*refs, q_blocks):
    qn_refs, qp_refs = refs[:q_blocks], refs[q_blocks:2 * q_blocks]
    kn_ref, kp_ref, vt_ref, o_ref, st_scr, p_scr = refs[2 * q_blocks:]
    kp = kp_ref[0]
    n_keys, n_q = p_scr.shape
    kb = ATTN_KEY_BLOCK
    sub = 8

    def scores(h):
        hc = slice(h * LANES, (h + 1) * LANES)
        qc = jnp.concatenate(
            [jnp.concatenate([qn[0, :, hc], qp[0, :, hc]], axis=1) for qn, qp in zip(qn_refs, qp_refs)], axis=0)
        kc = jnp.concatenate([kn_ref[0, :, hc], kp], axis=1)
        st_scr[h % 2] = _dot_nt(kc, qc)

    scores(0)
    for h in range(MLA_HEADS):
        if h + 1 < MLA_HEADS:
            scores(h + 1)
        st = st_scr.at[h % 2]
        hc = slice(h * LANES, (h + 1) * LANES)
        blocks = [slice(c * kb, (c + 1) * kb) for c in range(n_keys // kb)]
        m8 = None
        for rows in blocks:
            bm = jnp.max(st[rows, :].reshape(kb // sub, sub, n_q), axis=0)
            m8 = bm if m8 is None else jnp.maximum(m8, bm)
        m = jnp.max(m8, axis=0, keepdims=True)
        l8 = jnp.zeros((sub, n_q), F32)
        for rows in blocks:
            p = jnp.exp2(st[rows, :] - m)
            l8 = l8 + jnp.sum(p.reshape(kb // sub, sub, n_q), axis=0)
            p_scr[rows, :] = p.astype(BF16)
        denom = jnp.sum(l8, axis=0, keepdims=True)
        ot = _dot(vt_ref[0, hc, :], p_scr[...])
        o_ref[0, :, hc] = (ot / denom).T.astype(BF16)


def _attention(qn, qp, kn, kp, vt, *, n_ctx):
    b, s, w = qn.shape
    tb = MIX_TILE
    lat_q_blocks = 2
    tq = tb * lat_q_blocks
    assert n_ctx == tb and (s - n_ctx) % tq == 0 and MLA_V == LANES
    ctx_blocks = n_ctx // tb

    def call(q_blocks, n_keys, q_maps, n_tiles, name):
        rows = q_blocks * tb
        whole = lambda i, t: (i, 0, 0)
        est = (4 * _nbytes((rows, w), BF16) + 2 * (2 * _nbytes((n_keys, w), BF16) + _nbytes((n_keys, LANES), BF16))
               + 2 * _nbytes((rows, w), BF16) + 3 * _nbytes((n_keys, rows), F32) + _nbytes((n_keys, 2 * LANES), BF16))
        return pl.pallas_call(
            functools.partial(_attn_kernel, q_blocks=q_blocks),
            grid=(b, n_tiles),
            in_specs=[pl.BlockSpec((1, tb, w), m) for m in q_maps] * 2
                     + [pl.BlockSpec((1, n_keys, w), whole), pl.BlockSpec((1, n_keys, LANES), whole),
                        pl.BlockSpec((1, w, n_keys), whole)],
            out_specs=pl.BlockSpec((1, rows, w), lambda i, t: (i, t, 0)),
            out_shape=jax.ShapeDtypeStruct((b, n_tiles * rows, w), BF16),
            scratch_shapes=[pltpu.VMEM((2, n_keys, rows), F32), pltpu.VMEM((n_keys, rows), BF16)],
            compiler_params=_params(est, 2),
            name=name,
        )(*([qn] * q_blocks), *([qp] * q_blocks), kn, kp, vt)

    att_c = call(1, n_ctx, [lambda i, t: (i, 0, 0)], 1, "attention_ctx")
    lat_maps = [lambda i, t, j=j: (i, ctx_blocks + lat_q_blocks * t + j, 0) for j in range(lat_q_blocks)]
    att_l = call(lat_q_blocks, s, lat_maps, (s - n_ctx) // tq, "attention")
    return att_c, att_l


def _pool_bands(tile_rows):
    t = np.arange(tile_rows)[:, None]
    j = np.arange(tile_rows + 2 * HALO)[None, :]
    bands = []
    for win in POOL_WINDOWS:
        lo = HALO + t - win // 2
        bands.append(((j >= lo) & (j < lo + win)).astype(np.float32))
    return np.stack(bands)


def _merge_kernel(h_ref, ml_ref, mc_ref, yf_ref, yb_ref, zs_ref, attc_ref, attl_ref,
                  pv_ref, pvp_ref, pvn_ref, cv_ref, cvp_ref, cvn_ref, sg_ref,
                  snw_ref, wso, wmo, band_ref, pw_ref, ps_ref, wpo, cw_ref, wco, wo,
                  o_ref, ext_u, *, n_ctx):
    tm = h_ref.shape[1]
    d = h_ref.shape[2]
    t = pl.program_id(1)
    s_len = pl.num_programs(1) * tm
    mod = _row_mods(ml_ref, mc_ref, t, tm, n_ctx)
    seq_start, seq_end = _seq_edges(t, tm, n_ctx, s_len)

    def gate(k):
        return sg_ref[0, :, k * d:(k + 1) * d].astype(F32)

    mla_b = _dot(jnp.where(t < n_ctx // tm, attc_ref[0], attl_ref[0]), wmo[...])
    merged = gate(1) * mla_b

    pv = pv_ref[0]
    ext_p = jnp.concatenate([jnp.where(seq_start, jnp.zeros_like(pvp_ref[0]), pvp_ref[0]), pv,
                             jnp.where(seq_end, jnp.zeros_like(pvn_ref[0]), pvn_ref[0])], axis=0)
    rows = t * tm + lax.broadcasted_iota(jnp.int32, (tm, 1), 0)
    in_ctx = rows < n_ctx
    pos = jnp.where(in_ctx, rows, rows - n_ctx)
    seq_n = jnp.where(in_ctx, n_ctx, s_len - n_ctx)
    pooled = []
    for gi, win in enumerate(POOL_WINDOWS):
        gc = slice(gi * POOL_GROUP, (gi + 1) * POOL_GROUP)
        left = win // 2
        lo = jnp.clip(pos - left, 0, seq_n)
        hi = jnp.clip(pos - left + win, 0, seq_n)
        p_g = _dot(band_ref[gi], ext_p[:, gc]) / (hi - lo).astype(F32) - pv[:, gc].astype(F32)
        pooled.append(_dot(p_g.astype(BF16), pw_ref[gi]))
    pool_y = jnp.concatenate(pooled, axis=1) * ps_ref[...]
    merged = merged + gate(2) * _dot(pool_y.astype(BF16), wpo[...])

    y = yf_ref[0].astype(F32) + yb_ref[0].astype(F32)
    ssd_in = _rms(y * zs_ref[0].astype(F32), snw_ref[...]).astype(BF16)
    merged = merged + gate(0) * _dot(ssd_in, wso[...])

    def gated_in(ref):
        return ref[0, :, CONV_WIDTH:2 * CONV_WIDTH].astype(F32) * ref[0, :, 2 * CONV_WIDTH:].astype(F32)
    ext_u[0:HALO, :] = jnp.where(seq_start, 0.0, gated_in(cvp_ref))
    ext_u[HALO:HALO + tm, :] = gated_in(cv_ref)
    ext_u[HALO + tm:, :] = jnp.where(seq_end, 0.0, gated_in(cvn_ref))
    conv = cw_ref[0:1, :] * ext_u[HALO - 1:HALO - 1 + tm, :]
    for k in range(1, CONV_K):
        conv = conv + cw_ref[k:k + 1, :] * ext_u[HALO - 1 + k:HALO - 1 + k + tm, :]
    conv_in = (cv_ref[0, :, :CONV_WIDTH].astype(F32) * conv).astype(BF16)
    merged = merged + gate(3) * _dot(conv_in, wco[...])
    o_ref[0] = h_ref[0] + mod(5) * _dot(merged.astype(BF16), wo[...])


def _merge(h, ml, mc, yf, yb, zs, att_c, att_l, pv, cv, sg, w, *, n_ctx):
    b, s, d = h.shape
    tm = MIX_TILE
    assert s % tm == 0 and n_ctx % tm == 0 and tm % HALO == 0
    ctx_tiles = n_ctx // tm
    tile = lambda i, t: (i, t, 0)
    prev, nxt = _halo_maps(tm, s)
    bands = _pool_bands(tm)
    weights = [w["ssd_norm"], w["ssd_w_out"], w["mla_w_out"], jnp.asarray(bands, BF16), w["pool_w"],
               w["pool_scale"], w["pool_w_out"], w["sconv_w"], w["sconv_w_out"], w["w_o"]]
    stream = (2 * _nbytes((tm, d), F32) + _nbytes((tm, 4 * SSD_INNER + POOL_WIDTH + 3 * CONV_WIDTH + N_BRANCH * d), BF16)
              + 2 * _nbytes((HALO, POOL_WIDTH + 3 * CONV_WIDTH), BF16))
    est = (2 * stream + sum(_nbytes(x.shape, x.dtype) for x in weights)
           + _nbytes((tm + 2 * HALO, d), F32) + 10 * _nbytes((tm, d), F32))
    return pl.pallas_call(
        functools.partial(_merge_kernel, n_ctx=n_ctx),
        grid=(b, s // tm),
        in_specs=[pl.BlockSpec((1, tm, d), tile),
                  pl.BlockSpec((1, N_MOD, d), lambda i, t: (i, 0, 0)),
                  _resident(mc.shape),
                  pl.BlockSpec((1, tm, SSD_INNER), tile),
                  pl.BlockSpec((1, tm, SSD_INNER), tile),
                  pl.BlockSpec((1, tm, SSD_INNER), tile),
                  pl.BlockSpec((1, tm, MLA_HEADS * MLA_V), lambda i, t: (i, jnp.minimum(t, ctx_tiles - 1), 0)),
                  pl.BlockSpec((1, tm, MLA_HEADS * MLA_V), lambda i, t: (i, jnp.maximum(t - ctx_tiles, 0), 0)),
                  pl.BlockSpec((1, tm, POOL_WIDTH), tile),
                  pl.BlockSpec((1, HALO, POOL_WIDTH), prev),
                  pl.BlockSpec((1, HALO, POOL_WIDTH), nxt),
                  pl.BlockSpec((1, tm, 3 * CONV_WIDTH), tile),
                  pl.BlockSpec((1, HALO, 3 * CONV_WIDTH), prev),
                  pl.BlockSpec((1, HALO, 3 * CONV_WIDTH), nxt),
                  pl.BlockSpec((1, tm, N_BRANCH * d), tile)]
                 + [_resident(x.shape) for x in weights],
        out_specs=pl.BlockSpec((1, tm, d), tile),
        out_shape=jax.ShapeDtypeStruct(h.shape, F32),
        scratch_shapes=[pltpu.VMEM((tm + 2 * HALO, CONV_WIDTH), F32)],
        compiler_params=_params(est, 2),
        name="merge",
    )(h, ml, mc, yf, yb, zs, att_c, att_l, pv, pv, pv, cv, cv, cv, sg, *weights)


def _rope_tables(n_ctx, n_lat):
    rows = n_lat // GRID_W
    row = np.repeat(np.arange(rows), GRID_W).astype(np.float32)
    col = np.tile(np.arange(GRID_W), rows).astype(np.float32)
    half = MLA_ROPE // 2
    inv = (1.0 / (ROPE_THETA ** (jnp.arange(0, half, 2, dtype=F32) / half)))
    ar = jnp.asarray(row)[:, None] * inv
    ac = jnp.asarray(col)[:, None] * inv
    ang = jnp.concatenate([ar, ar, ac, ac], axis=-1)
    pad = LANES - MLA_ROPE
    cos = jnp.concatenate([jnp.ones((n_ctx, MLA_ROPE), F32), jnp.cos(ang)], axis=0)
    sin = jnp.concatenate([jnp.zeros((n_ctx, MLA_ROPE), F32), jnp.sin(ang)], axis=0)
    return jnp.pad(cos, ((0, 0), (0, pad))), jnp.pad(sin, ((0, 0), (0, pad)))


def _rot_cols(w):
    w1, w2, w3, w4 = jnp.split(w, 4, axis=-1)
    return jnp.concatenate([-w2, w1, -w4, w3], axis=-1)


def _layer_weights(i, p):
    w_in = p["w_in"][i]
    splits = np.cumsum([SSD_INNER, SSD_XBC, 2 * SSD_HEADS, MLA_Q_RANK, MLA_KV_RANK, MLA_ROPE,
                        POOL_WIDTH, 3 * CONV_WIDTH])
    wz, wxbc, wdt, wqd, wkvd, wkr, wpv, wcv, wg = jnp.split(w_in, splits, axis=1)
    lane_pad = lambda x: jnp.pad(x, ((0, 0), (0, LANES - x.shape[1])))
    uq = p["mla_w_uq"][i].reshape(MLA_Q_RANK, MLA_HEADS, MLA_QK)
    uq_pe = uq[:, :, MLA_NOPE:]
    head_pad = lambda x: jnp.pad(x, ((0, 0), (0, 0), (0, LANES - MLA_ROPE))).reshape(MLA_Q_RANK, MLA_HEADS * LANES)
    bf = lambda x: x.astype(BF16)
    return {
        "wz": bf(wz), "wxbc": bf(wxbc),
        "wsm": bf(jnp.concatenate([wqd, wkvd, lane_pad(wkr), lane_pad(_rot_cols(wkr)), lane_pad(wdt)], axis=1)),
        "q_norm": p["mla_q_norm"][i].reshape(1, -1), "kv_norm": p["mla_kv_norm"][i].reshape(1, -1),
        "wuqn": bf(uq[:, :, :MLA_NOPE].reshape(MLA_Q_RANK, MLA_HEADS * MLA_NOPE)),
        "wuqp": bf(head_pad(uq_pe)), "wuqr": bf(head_pad(_rot_cols(uq_pe))),
        "wuk": bf(p["mla_w_uk"][i]), "wuvt": bf(p["mla_w_uv"][i].T),
        "wpv": bf(wpv), "wcv": bf(wcv), "wg": bf(wg),
        "ssd_norm": p["ssd_norm"][i].reshape(1, -1), "ssd_w_out": bf(p["ssd_w_out"][i]),
        "mla_w_out": bf(p["mla_w_out"][i]), "pool_w": bf(p["pool_w"][i]),
        "pool_scale": p["pool_scale"][i].reshape(1, -1), "pool_w_out": bf(p["pool_w_out"][i]),
        "sconv_w": p["sconv_w"][i], "sconv_w_out": bf(p["sconv_w_out"][i]), "w_o": bf(p["w_o"][i]),
    }


def kernel(x, c, ctx, c_ctx, w_ada, b_ada, ffn1_norm, ffn1_w_gate, ffn1_w_up, ffn1_w_down, mix_norm, w_in, ssd_conv_w, ssd_conv_b, ssd_dt_bias, ssd_a_log, ssd_d, ssd_norm, ssd_w_out, mla_q_norm, mla_w_uq, mla_kv_norm, mla_w_uk, mla_w_uv, mla_w_out, pool_w, pool_scale, pool_w_out, sconv_w, sconv_w_out, w_o, ffn2_norm, ffn2_w_gate, ffn2_w_up, ffn2_w_down, final_norm):
    p = dict(w_in=w_in, ssd_norm=ssd_norm, ssd_w_out=ssd_w_out, mla_q_norm=mla_q_norm, mla_w_uq=mla_w_uq,
             mla_kv_norm=mla_kv_norm, mla_w_uk=mla_w_uk, mla_w_uv=mla_w_uv, mla_w_out=mla_w_out,
             pool_w=pool_w, pool_scale=pool_scale, pool_w_out=pool_w_out, sconv_w=sconv_w,
             sconv_w_out=sconv_w_out, w_o=w_o)
    bsz, n_lat, d = x.shape
    n_ctx = ctx.shape[1]
    depth = w_ada.shape[0]

    mod_rows = -(-(bsz + 1) // 8) * 8
    cc = jnp.concatenate([c, c_ctx[None, :], jnp.zeros((mod_rows - bsz - 1, d), F32)], axis=0)
    mods = _adaln(cc, w_ada, b_ada).reshape(depth, mod_rows, N_MOD, d)

    cos_t, sin_t = _rope_tables(n_ctx, n_lat)
    ffn1_w = [a.astype(BF16) for a in (ffn1_w_gate, ffn1_w_up, ffn1_w_down)]
    ffn2_w = [a.astype(BF16) for a in (ffn2_w_gate, ffn2_w_up, ffn2_w_down)]
    h = x
    for i in range(depth):
        ml, mc = mods[i, :bsz], mods[i, bsz]
        w = _layer_weights(i, p)
        h = _ffn(h, ml, mc, ffn1_norm[i], *ffn1_w, i, n_ctx=n_ctx, mod0=0, ctx=ctx if i == 0 else None)
        zs, xs, cm, bt, dt, qn, qp, kn, kp, vt = _proj_seq(
            h, ml, mc, mix_norm[i], cos_t, sin_t, ssd_conv_w[i], ssd_conv_b[i], w, n_ctx=n_ctx)
        pv, cv, sg = _proj_loc(h, ml, mc, mix_norm[i], w, n_ctx=n_ctx)
        yf, yb = _ssd_scan(xs, cm, bt, dt, ssd_dt_bias[i], ssd_a_log[i], ssd_d[i], n_ctx=n_ctx)
        att_c, att_l = _attention(qn, qp, kn, kp, vt, n_ctx=n_ctx)
        h = _merge(h, ml, mc, yf, yb, zs, att_c, att_l, pv, cv, sg, w, n_ctx=n_ctx)
        h = _ffn(h, ml, mc, ffn2_norm[i], *ffn2_w, i, n_ctx=n_ctx, mod0=6,
                 final_w=final_norm if i == depth - 1 else None)
    return h
```

```python
import functools
import math

import jax
import jax.numpy as jnp
import numpy as np
from jax import lax
from jax.experimental import pallas as pl
from jax.experimental.pallas import tpu as pltpu

F32 = jnp.float32
BF16 = jnp.bfloat16

EPS = 1e-6
LOG2E = math.log2(math.e)
FFN_RES = 0.5
N_MOD = 9
GRID_W = 64
ROPE_THETA = 10000.0

SSD_HEADS = 16
SSD_HEAD_DIM = 64
SSD_GROUPS = 4
SSD_STATE = 128
SSD_CONV = 4
SSD_CHUNK = 128
SSD_INNER = SSD_HEADS * SSD_HEAD_DIM
SSD_BC = SSD_GROUPS * SSD_STATE
SSD_XBC = SSD_INNER + 2 * SSD_BC
HEADS_PER_GROUP = SSD_HEADS // SSD_GROUPS

MLA_HEADS = 8
MLA_Q_RANK = 384
MLA_KV_RANK = 256
MLA_NOPE = 128
MLA_ROPE = 64
MLA_V = 128
MLA_QK = MLA_NOPE + MLA_ROPE

POOL_WINDOWS = (2, 4, 8, 16)
POOL_GROUP = 256
POOL_WIDTH = POOL_GROUP * len(POOL_WINDOWS)
CONV_WIDTH = 1024
CONV_K = 3
N_BRANCH = 4

LANES = 128
BF16_ROWS = 16
HALO = BF16_ROWS
VMEM_CAP = 56 * 1024 * 1024
VMEM_SLACK = 12 * 1024 * 1024

ROW_TILE = 768
MIX_TILE = 256
FFN_CHUNK = 256
ATTN_KEY_BLOCK = 256


def _params(est_bytes, n_axes):
    return pltpu.CompilerParams(
        dimension_semantics=("arbitrary",) * n_axes,
        vmem_limit_bytes=int(min(est_bytes + VMEM_SLACK, VMEM_CAP)))


def _resident(shape):
    nd = len(shape)
    return pl.BlockSpec(shape, lambda *_: (0,) * nd, pipeline_mode=pl.Buffered(1))


def _layer_resident(x, layer):
    nd = x.ndim - 1
    return pl.BlockSpec((None,) + x.shape[1:], lambda *_: (layer,) + (0,) * nd, pipeline_mode=pl.Buffered(1))


def _nbytes(shape, dtype):
    return int(np.prod(shape)) * jnp.dtype(dtype).itemsize


def _dot(a, b):
    return jnp.dot(a, b, preferred_element_type=F32)


def _dot_nt(a, b):
    return lax.dot_general(a, b, (((1,), (1,)), ((), ())), preferred_element_type=F32)


def _silu(x):
    return x * jax.nn.sigmoid(x)


def _rms(x, w):
    ms = jnp.mean(x * x, axis=-1, keepdims=True)
    return x * lax.rsqrt(ms + EPS) * w


def _row_mods(ml_ref, mc_ref, tile, tile_rows, n_ctx):
    rows = tile * tile_rows + lax.broadcasted_iota(jnp.int32, (tile_rows, 1), 0)
    is_ctx = rows < n_ctx

    def mod(k):
        return jnp.where(is_ctx, mc_ref[k:k + 1, :], ml_ref[0, k:k + 1, :])
    return mod


def _halo_maps(tile_rows, s_len):
    hb = tile_rows // HALO
    last = s_len // HALO - 1
    prev = lambda i, t: (i, jnp.maximum(t * hb - 1, 0), 0)
    nxt = lambda i, t: (i, jnp.minimum((t + 1) * hb, last), 0)
    return prev, nxt


def _seq_edges(tile, tile_rows, n_ctx, s_len):
    row0 = tile * tile_rows
    start = jnp.logical_or(row0 == 0, row0 == n_ctx)
    end = jnp.logical_or(row0 + tile_rows == n_ctx, row0 + tile_rows == s_len)
    return start, end


def _ada_kernel(c_ref, w_ref, b_ref, o_ref):
    s = _silu(c_ref[...]).astype(BF16)
    o_ref[0] = _dot(s, w_ref[0].astype(BF16)) + b_ref[0]


def _adaln(cc, w_ada, b_ada):
    depth, d, n = w_ada.shape
    rows = cc.shape[0]
    tn = n // 8
    est = 2 * (_nbytes((d, tn), F32) + _nbytes((rows, tn), F32)) + _nbytes((d, tn), BF16)
    return pl.pallas_call(
        _ada_kernel,
        grid=(depth, n // tn),
        in_specs=[pl.BlockSpec((rows, d), lambda i, j: (0, 0)),
                  pl.BlockSpec((1, d, tn), lambda i, j: (i, 0, j)),
                  pl.BlockSpec((1, 1, tn), lambda i, j: (i, 0, j))],
        out_specs=pl.BlockSpec((1, rows, tn), lambda i, j: (i, 0, j)),
        out_shape=jax.ShapeDtypeStruct((depth, rows, n), F32),
        compiler_params=_params(est, 2),
        name="adaln",
    )(cc, w_ada, b_ada.reshape(depth, 1, n))


def _ffn_body(h, mod, mod0, nw_ref, wg_ref, wu_ref, wd_ref, p_scr):
    d_ff = wg_ref.shape[1]
    a = (_rms(h, nw_ref[...]) * (1.0 + mod(mod0 + 1)) + mod(mod0)).astype(BF16)
    for j in range(d_ff // FFN_CHUNK):
        cols = slice(j * FFN_CHUNK, (j + 1) * FFN_CHUNK)
        g = _dot(a, wg_ref[:, cols])
        u = _dot(a, wu_ref[:, cols])
        p_scr[:, cols] = (_silu(g) * u).astype(BF16)
    y = _dot(p_scr[...], wd_ref[...])
    return h + (FFN_RES * mod(mod0 + 2)) * y


def _ffn_kernel(h_ref, ml_ref, mc_ref, nw_ref, wg_ref, wu_ref, wd_ref, o_ref, p_scr, *, n_ctx, mod0):
    tm = h_ref.shape[1]
    mod = _row_mods(ml_ref, mc_ref, pl.program_id(1), tm, n_ctx)
    o_ref[0] = _ffn_body(h_ref[0], mod, mod0, nw_ref, wg_ref, wu_ref, wd_ref, p_scr)


def _ffn_first_kernel(ctx_ref, x_ref, ml_ref, mc_ref, nw_ref, wg_ref, wu_ref, wd_ref, o_ref, p_scr, *, n_ctx, mod0):
    tm = x_ref.shape[1]
    t = pl.program_id(1)
    mod = _row_mods(ml_ref, mc_ref, t, tm, n_ctx)
    h = jnp.where(t < n_ctx // tm, ctx_ref[0], x_ref[0])
    o_ref[0] = _ffn_body(h, mod, mod0, nw_ref, wg_ref, wu_ref, wd_ref, p_scr)


def _ffn_last_kernel(h_ref, ml_ref, mc_ref, nw_ref, wg_ref, wu_ref, wd_ref, fw_ref, o_ref, p_scr, *, n_ctx, mod0):
    tm = h_ref.shape[1]
    mod = _row_mods(ml_ref, mc_ref, pl.program_id(1), tm, n_ctx)
    o_ref[0] = _rms(_ffn_body(h_ref[0], mod, mod0, nw_ref, wg_ref, wu_ref, wd_ref, p_scr), fw_ref[...])


def _ffn(h, ml, mc, norm_w, wg, wu, wd, layer, *, n_ctx, mod0, ctx=None, final_w=None):
    b, _, d = h.shape
    d_ff = wg.shape[2]
    layer_block = lambda x: _layer_resident(x, layer)
    special = ctx is not None or final_w is not None
    tm = MIX_TILE if special else ROW_TILE
    s = h.shape[1] + (ctx.shape[1] if ctx is not None else 0)
    assert s % tm == 0 and d_ff % FFN_CHUNK == 0 and (not special or n_ctx % tm == 0)
    ctx_tiles = n_ctx // tm
    est = (4 * _nbytes((tm, d), F32) + 3 * _nbytes((d, d_ff), BF16) + _nbytes((tm, d_ff), BF16)
           + 4 * _nbytes((tm, FFN_CHUNK), F32) + 2 * _nbytes((tm, d), F32))
    tile = lambda i, t: (i, t, 0)
    lat_tile = lambda i, t: (i, jnp.maximum(t - ctx_tiles, 0), 0)
    common = [pl.BlockSpec((1, N_MOD, d), lambda i, t: (i, 0, 0)),
              _resident(mc.shape), _resident((1, d)),
              layer_block(wg), layer_block(wu), layer_block(wd)]
    args = [ml, mc, norm_w.reshape(1, d), wg, wu, wd]
    out_rows, out_spec = s, pl.BlockSpec((1, tm, d), tile)
    if ctx is not None:
        body = _ffn_first_kernel
        in_specs = [pl.BlockSpec((1, tm, d), lambda i, t: (i, jnp.minimum(t, ctx_tiles - 1), 0)),
                    pl.BlockSpec((1, tm, d), lat_tile)] + common
        args = [ctx, h] + args
    elif final_w is not None:
        body = _ffn_last_kernel
        in_specs = [pl.BlockSpec((1, tm, d), tile)] + common + [_resident((1, d))]
        args = [h] + args + [final_w.reshape(1, d)]
        out_rows, out_spec = s - n_ctx, pl.BlockSpec((1, tm, d), lat_tile)
    else:
        body = _ffn_kernel
        in_specs = [pl.BlockSpec((1, tm, d), tile)] + common
        args = [h] + args
    return pl.pallas_call(
        functools.partial(body, n_ctx=n_ctx, mod0=mod0),
        grid=(b, s // tm),
        in_specs=in_specs,
        out_specs=out_spec,
        out_shape=jax.ShapeDtypeStruct((b, out_rows, d), F32),
        scratch_shapes=[pltpu.VMEM((tm, d_ff), BF16)],
        compiler_params=_params(est, 2),
        name="ffn",
    )(*args)


def _proj_seq_kernel(h_ref, hp_ref, hn_ref, ml_ref, mc_ref, nw_ref, cos_ref, sin_ref, cw_ref, cb_ref,
                     wz, wxbc, wsm, qnw_ref, kvnw_ref, wuqn, wuqp, wuqr, wuk, wuvt,
                     zs_o, xs_o, c_o, bt_o, dt_o, qn_o, qp_o, kn_o, kp_o, vt_o, *, n_ctx, qk_scale):
    tm = h_ref.shape[1]
    t = pl.program_id(1)
    s_len = pl.num_programs(1) * tm
    row0 = t * tm

    def normed(ref, first_row):
        rows = first_row + lax.broadcasted_iota(jnp.int32, (ref.shape[1], 1), 0)
        is_ctx = rows < n_ctx
        mod = lambda k: jnp.where(is_ctx, mc_ref[k:k + 1, :], ml_ref[0, k:k + 1, :])
        return (_rms(ref[0], nw_ref[...]) * (1.0 + mod(4)) + mod(3)).astype(BF16)

    a = normed(h_ref, row0)
    a_ext = jnp.concatenate([normed(hp_ref, row0 - HALO), a, normed(hn_ref, row0 + tm)], axis=0)
    xbc = _dot(a_ext, wxbc[...])
    zs_o[0] = _silu(_dot(a, wz[...])).astype(BF16)
    sm = _dot(a, wsm[...])
    c0 = MLA_Q_RANK + MLA_KV_RANK
    qd, kvd, kr, kr_rot = sm[:, :MLA_Q_RANK], sm[:, MLA_Q_RANK:c0], sm[:, c0:c0 + LANES], sm[:, c0 + LANES:c0 + 2 * LANES]
    dt_o[0] = sm[:, c0 + 2 * LANES:]
    cos = cos_ref[...]
    sin = sin_ref[...]
    cos_h = jnp.concatenate([cos] * MLA_HEADS, axis=1)
    sin_h = jnp.concatenate([sin] * MLA_HEADS, axis=1)
    qn = _rms(qd, qnw_ref[...]).astype(BF16)
    qn_o[0] = (_dot(qn, wuqn[...]) * qk_scale).astype(BF16)
    qp_o[0] = ((_dot(qn, wuqp[...]) * cos_h + _dot(qn, wuqr[...]) * sin_h) * qk_scale).astype(BF16)
    ckv = _rms(kvd, kvnw_ref[...]).astype(BF16)
    kn_o[0] = _dot(ckv, wuk[...]).astype(BF16)
    vt_o[0] = _dot_nt(wuvt[...], ckv).astype(BF16)
    kp_o[0] = (kr * cos + kr_rot * sin).astype(BF16)

    ext_rows = tm + 2 * HALO
    pad_l = SSD_CONV // 2
    rows = row0 + lax.broadcasted_iota(jnp.int32, (tm, LANES), 0)
    in_ctx = rows < n_ctx
    pos = jnp.where(in_ctx, rows, rows - n_ctx)
    seq_n = jnp.where(in_ctx, n_ctx, s_len - n_ctx)
    valid = [pos + (k - pad_l) >= 0 if k < pad_l else pos + (k - pad_l) < seq_n for k in range(SSD_CONV)]
    sub = 8
    starts = sorted({0, n_ctx % tm // sub * sub})
    ends = sorted({(n_ctx - 1) % tm // sub * sub, (tm - 1) // sub * sub})

    def masked(tap, k):
        pieces, r = [], 0
        for g in (starts if k < pad_l else ends):
            pieces += [tap[r:g], jnp.where(valid[k][g:g + sub], tap[g:g + sub], 0.0)]
            r = g + sub
        return jnp.concatenate([p for p in pieces + [tap[r:]] if p.shape[0]], axis=0)

    q, n = SSD_CHUNK, SSD_STATE
    for j in range(SSD_XBC // LANES):
        lc = slice(j * LANES, (j + 1) * LANES)
        ext = xbc[:, lc]
        acc = cb_ref[:, lc] + cw_ref[pad_l:pad_l + 1, lc] * ext[HALO:HALO + tm]
        for k in range(SSD_CONV):
            if k != pad_l:
                tap = pltpu.roll(ext, (pad_l - k) % ext_rows, axis=0)[HALO:HALO + tm]
                acc = acc + cw_ref[k:k + 1, lc] * masked(tap, k)
        act = _silu(acc)
        col = j * LANES
        if col < SSD_INNER:
            xs_o[0, :, lc] = act.astype(BF16)
        elif col < SSD_INNER + SSD_BC:
            g = (col - SSD_INNER) // n
            for ci in range(tm // q):
                r0 = (ci * SSD_GROUPS + g) * n
                bt_o[0, r0:r0 + n, :] = act[ci * q:(ci + 1) * q, :].T.astype(BF16)
        else:
            cc = col - SSD_INNER - SSD_BC
            c_o[0, :, cc:cc + LANES] = act.astype(BF16)


def _proj_seq(h, ml, mc, norm_w, cos_t, sin_t, conv_w, conv_b, w, *, n_ctx):
    b, s, d = h.shape
    tm = ROW_TILE
    q = SSD_CHUNK
    assert s % tm == 0 and tm % q == 0 and tm % HALO == 0 and n_ctx % HALO == 0 and SSD_STATE == LANES
    assert MLA_Q_RANK % LANES == 0 and MLA_KV_RANK % LANES == 0
    weights = [w["wz"], w["wxbc"], w["wsm"], w["q_norm"], w["kv_norm"],
               w["wuqn"], w["wuqp"], w["wuqr"], w["wuk"], w["wuvt"]]
    widths = [(SSD_INNER, BF16), (SSD_INNER, BF16), (SSD_BC, BF16), None, (LANES, F32),
              (MLA_HEADS * MLA_NOPE, BF16), (MLA_HEADS * LANES, BF16), (MLA_HEADS * MLA_NOPE, BF16), (LANES, BF16)]
    tile = lambda i, t: (i, t, 0)
    prev, nxt = _halo_maps(tm, s)
    vw = MLA_HEADS * MLA_V
    bt_rows = tm // q * SSD_BC
    row_outs = [x for x in widths if x is not None]
    est = (2 * _nbytes((tm + 2 * HALO, d), F32) + sum(_nbytes(x.shape, x.dtype) for x in weights)
           + 2 * sum(_nbytes((tm, n), dt) for n, dt in row_outs) + 2 * _nbytes((vw, tm), BF16)
           + 2 * _nbytes((bt_rows, q), BF16) + 4 * _nbytes((tm + 2 * HALO, SSD_XBC), F32))
    out_specs, out_shape = [], []
    for x in widths:
        if x is None:
            out_specs.append(pl.BlockSpec((1, bt_rows, q), tile))
            out_shape.append(jax.ShapeDtypeStruct((b, s // q * SSD_BC, q), BF16))
        else:
            out_specs.append(pl.BlockSpec((1, tm, x[0]), tile))
            out_shape.append(jax.ShapeDtypeStruct((b, s, x[0]), x[1]))
    return pl.pallas_call(
        functools.partial(_proj_seq_kernel, n_ctx=n_ctx, qk_scale=MLA_QK ** -0.5 * LOG2E),
        grid=(b, s // tm),
        in_specs=[pl.BlockSpec((1, tm, d), tile),
                  pl.BlockSpec((1, HALO, d), prev),
                  pl.BlockSpec((1, HALO, d), nxt),
                  pl.BlockSpec((1, N_MOD, d), lambda i, t: (i, 0, 0)),
                  _resident(mc.shape), _resident((1, d)),
                  pl.BlockSpec((tm, LANES), lambda i, t: (t, 0)),
                  pl.BlockSpec((tm, LANES), lambda i, t: (t, 0)),
                  _resident(conv_w.shape), _resident((1, SSD_XBC))]
                 + [_resident(x.shape) for x in weights],
        out_specs=out_specs + [pl.BlockSpec((1, vw, tm), lambda i, t: (i, 0, t))],
        out_shape=out_shape + [jax.ShapeDtypeStruct((b, vw, s), BF16)],
        compiler_params=_params(est, 2),
        name="proj_seq",
    )(h, h, h, ml, mc, norm_w.reshape(1, d), cos_t, sin_t, conv_w, conv_b.reshape(1, -1), *weights)


def _proj_loc_kernel(h_ref, ml_ref, mc_ref, nw_ref, wpv, wcv, wg, pv_o, cv_o, g_o, *, n_ctx):
    tm = h_ref.shape[1]
    mod = _row_mods(ml_ref, mc_ref, pl.program_id(1), tm, n_ctx)
    a = (_rms(h_ref[0], nw_ref[...]) * (1.0 + mod(4)) + mod(3)).astype(BF16)
    pv_o[0] = _dot(a, wpv[...]).astype(BF16)
    cv_o[0] = _dot(a, wcv[...]).astype(BF16)
    g_o[0] = jax.nn.sigmoid(_dot(a, wg[...])).astype(BF16)


def _proj_loc(h, ml, mc, norm_w, w, *, n_ctx):
    b, s, d = h.shape
    tm = ROW_TILE
    weights = [w["wpv"], w["wcv"], w["wg"]]
    widths = [x.shape[1] for x in weights]
    tile = lambda i, t: (i, t, 0)
    est = (2 * _nbytes((tm, d), F32) + sum(_nbytes(x.shape, BF16) for x in weights)
           + 2 * sum(_nbytes((tm, n), BF16) for n in widths) + 2 * _nbytes((tm, max(widths)), F32))
    return pl.pallas_call(
        functools.partial(_proj_loc_kernel, n_ctx=n_ctx),
        grid=(b, s // tm),
        in_specs=[pl.BlockSpec((1, tm, d), tile),
                  pl.BlockSpec((1, N_MOD, d), lambda i, t: (i, 0, 0)),
                  _resident(mc.shape), _resident((1, d))]
                 + [_resident(x.shape) for x in weights],
        out_specs=[pl.BlockSpec((1, tm, n), tile) for n in widths],
        out_shape=[jax.ShapeDtypeStruct((b, s, n), BF16) for n in widths],
        compiler_params=_params(est, 2),
        name="proj_loc",
    )(h, ml, mc, norm_w.reshape(1, d), *weights)


def _split_bf16(x, parts):
    out = []
    for _ in range(parts):
        p = x.astype(BF16)
        out.append(p)
        x = x - p.astype(F32)
    return out


class _SsdChunk:
    def __init__(self, d, sub, xs_ref, c_ref, bt_ref, dt_ref, y_ref, st_ref, bias_ref, alog_ref, dskip_ref,
                 expand_ref, tri_ref):
        self.d = d
        self.rows = slice(sub * SSD_CHUNK, (sub + 1) * SSD_CHUNK)
        self.bt_row0 = sub * SSD_BC
        self.xs_ref, self.c_ref, self.bt_ref, self.dt_ref = xs_ref, c_ref, bt_ref, dt_ref
        self.y_ref, self.st_ref = y_ref, st_ref
        self.bias_ref, self.alog_ref, self.dskip_ref = bias_ref, alog_ref, dskip_ref
        self.expand_ref, self.tri_ref = expand_ref, tri_ref

    def cumsum(self):
        x = self.dt_ref[0, self.rows, :] + self.bias_ref[...]
        self.dtv = jnp.maximum(x, 0.0) + jnp.log1p(jnp.exp(-jnp.abs(x)))
        da = self.dtv * (-LOG2E * jnp.exp(self.alog_ref[...]))
        tri = self.tri_ref[self.d]
        self.cs = sum(_dot(tri, p) for p in _split_bf16(da, 3))

    def expand(self):
        q = SSD_CHUNK
        last = q - 1 if self.d == 0 else 0
        cs, dtv = self.cs, self.dtv
        tot = cs[last:last + 1, :]
        self.src_t = (cs - jnp.log2(dtv)).T
        stack = jnp.concatenate(
            [dtv * jnp.exp2(tot - cs), jnp.exp2(cs), jnp.broadcast_to(jnp.exp2(tot), (BF16_ROWS, LANES))], axis=0)
        ex = _dot(stack.astype(BF16), self.expand_ref[self.d])
        self.ecs_exp = ex[q:2 * q]
        self.etot_exp = ex[2 * q:2 * q + 1]
        self.xs = self.xs_ref[0, self.rows, :]
        self.xe = (self.xs.astype(F32) * ex[:q]).astype(BF16)

    def group(self, g):
        d, q, n = self.d, SSD_CHUNK, SSD_STATE
        gw = HEADS_PER_GROUP * SSD_HEAD_DIM
        li = lax.broadcasted_iota(jnp.int32, (q, q), 0)
        si = lax.broadcasted_iota(jnp.int32, (q, q), 1)
        causal = (li >= si) if d == 0 else (li <= si)
        head_of_lane = lax.broadcasted_iota(jnp.int32, (q, gw), 1) // SSD_HEAD_DIM
        gc = slice(g * gw, (g + 1) * gw)
        c_g = self.c_ref[0, self.rows, g * n:(g + 1) * n]
        bt_g = self.bt_ref[0, self.bt_row0 + g * n:self.bt_row0 + (g + 1) * n, :]
        cb = _dot(c_g, bt_g)
        st_g = self.st_ref[:, gc]
        y_off = _dot(c_g, st_g.astype(BF16))
        xs_g = self.xs[:, gc]
        m_parts, x_parts = [], []
        for r in range(HEADS_PER_GROUP):
            col = d * SSD_HEADS + g * HEADS_PER_GROUP + r
            seg = self.cs[:, col:col + 1] - self.src_t[col:col + 1, :]
            m_parts.append((cb * jnp.exp2(jnp.where(causal, seg, -jnp.inf))).astype(BF16))
            x_parts.append(jnp.where(head_of_lane == r, xs_g, jnp.zeros_like(xs_g)))
        y_g = _dot(jnp.concatenate(m_parts, axis=1), jnp.concatenate(x_parts, axis=0))
        y_g = y_g + y_off * self.ecs_exp[:, gc]
        if d == 0:
            y_g = y_g + xs_g.astype(F32) * self.dskip_ref[:, gc]
        self.y_ref[0, self.rows, gc] = y_g.astype(BF16)
        self.st_ref[:, gc] = st_g * self.etot_exp[:, gc] + _dot(bt_g, self.xe[:, gc])


SCAN_CHUNKS = 2


def _ssd_scan_kernel(xs_f, c_f, bt_f, dt_f, xs_b, c_b, bt_b, dt_b, bias_ref, alog_ref, dskip_ref, expand_ref,
                     tri_ref, yf_ref, yb_ref, stf_s, stb_s):
    @pl.when(pl.program_id(1) == 0)
    def _reset():
        stf_s[...] = jnp.zeros(stf_s.shape, F32)
        stb_s[...] = jnp.zeros(stb_s.shape, F32)
    shared = (bias_ref, alog_ref, dskip_ref, expand_ref, tri_ref)
    waves = [(_SsdChunk(0, k, xs_f, c_f, bt_f, dt_f, yf_ref, stf_s, *shared),
              _SsdChunk(1, SCAN_CHUNKS - 1 - k, xs_b, c_b, bt_b, dt_b, yb_ref, stb_s, *shared))
             for k in range(SCAN_CHUNKS)]
    for wave in waves:
        for ch in wave:
            ch.cumsum()
    for wave in waves:
        for ch in wave:
            ch.expand()
    for wave in waves:
        for g in range(SSD_GROUPS):
            for ch in wave:
                ch.group(g)


def _ssd_scan(xs, cm, bt, dt, dt_bias, a_log, d_skip, *, n_ctx):
    b, s, _ = xs.shape
    q = SSD_CHUNK
    rows = SCAN_CHUNKS * q
    assert s % rows == 0 and n_ctx % rows == 0
    nb = s // rows
    nbt = n_ctx // rows
    pad = LANES - 2 * SSD_HEADS
    bias_row = jnp.pad(dt_bias.reshape(1, -1), ((0, 0), (0, pad)))
    alog_row = jnp.pad(a_log.reshape(1, -1), ((0, 0), (0, pad)))
    dskip_row = jnp.repeat(d_skip, SSD_HEAD_DIM).reshape(1, SSD_INNER)
    e = np.zeros((2, LANES, SSD_INNER), np.float32)
    for d in range(2):
        for h in range(SSD_HEADS):
            e[d, d * SSD_HEADS + h, h * SSD_HEAD_DIM:(h + 1) * SSD_HEAD_DIM] = 1.0
    low = np.tril(np.ones((q, q), np.float32))
    tri = np.stack([low, low.T])
    fwd = lambda i, c: (i, c, 0)
    bwd = lambda i, c: (i, jnp.where(c < nbt, nbt - 1 - c, nb - 1 - c + nbt), 0)
    per_dir = lambda m: [pl.BlockSpec((1, rows, SSD_INNER), m), pl.BlockSpec((1, rows, SSD_BC), m),
                         pl.BlockSpec((1, SCAN_CHUNKS * SSD_BC, q), m), pl.BlockSpec((1, rows, LANES), m)]
    est = (4 * (_nbytes((rows, SSD_INNER + SSD_BC), BF16) + _nbytes((SCAN_CHUNKS * SSD_BC, q), BF16)
                + _nbytes((rows, LANES), F32))
           + 4 * _nbytes((rows, SSD_INNER), BF16) + 2 * _nbytes((SSD_STATE, SSD_INNER), F32)
           + _nbytes(e.shape, BF16) + 12 * SCAN_CHUNKS * _nbytes((q, SSD_INNER), F32))
    return pl.pallas_call(
        _ssd_scan_kernel,
        grid=(b, nb),
        in_specs=per_dir(fwd) + per_dir(bwd)
                 + [_resident((1, LANES)), _resident((1, LANES)), _resident((1, SSD_INNER)),
                    _resident(e.shape), _resident(tri.shape)],
        out_specs=[pl.BlockSpec((1, rows, SSD_INNER), fwd), pl.BlockSpec((1, rows, SSD_INNER), bwd)],
        out_shape=[jax.ShapeDtypeStruct((b, s, SSD_INNER), BF16)] * 2,
        scratch_shapes=[pltpu.VMEM((SSD_STATE, SSD_INNER), F32)] * 2,
        compiler_params=_params(est, 2),
        name="ssd_scan",
    )(xs, cm, bt, dt, xs, cm, bt, dt, bias_row, alog_row, dskip_row, jnp.asarray(e, BF16), jnp.asarray(tri, BF16))


def _attn_kernel(*refs, q_blocks):
    qn_refs, qp_refs = refs[:q_blocks], refs[q_blocks:2 * q_blocks]
    kn_ref, kp_ref, vt_ref, o_ref, st_scr, p_scr = refs[2 * q_blocks:]
    kp = kp_ref[0]
    n_keys, n_q = p_scr.shape
    kb = ATTN_KEY_BLOCK
    sub = 8

    def scores(h):
        hc = slice(h * LANES, (h + 1) * LANES)
        qc = jnp.concatenate(
            [jnp.concatenate([qn[0, :, hc], qp[0, :, hc]], axis=1) for qn, qp in zip(qn_refs, qp_refs)], axis=0)
        kc = jnp.concatenate([kn_ref[0, :, hc], kp], axis=1)
        st_scr[h % 2] = _dot_nt(kc, qc)

    scores(0)
    for h in range(MLA_HEADS):
        if h + 1 < MLA_HEADS:
            scores(h + 1)
        st = st_scr.at[h % 2]
        hc = slice(h * LANES, (h + 1) * LANES)
        blocks = [slice(c * kb, (c + 1) * kb) for c in range(n_keys // kb)]
        m8 = None
        for rows in blocks:
            bm = jnp.max(st[rows, :].reshape(kb // sub, sub, n_q), axis=0)
            m8 = bm if m8 is None else jnp.maximum(m8, bm)
        m = jnp.max(m8, axis=0, keepdims=True)
        l8 = jnp.zeros((sub, n_q), F32)
        for rows in blocks:
            p = jnp.exp2(st[rows, :] - m)
            l8 = l8 + jnp.sum(p.reshape(kb // sub, sub, n_q), axis=0)
            p_scr[rows, :] = p.astype(BF16)
        denom = jnp.sum(l8, axis=0, keepdims=True)
        ot = _dot(vt_ref[0, hc, :], p_scr[...])
        o_ref[0, :, hc] = (ot / denom).T.astype(BF16)


def _attention(qn, qp, kn, kp, vt, *, n_ctx):
    b, s, w = qn.shape
    tb = MIX_TILE
    lat_q_blocks = 2
    tq = tb * lat_q_blocks
    assert n_ctx == tb and (s - n_ctx) % tq == 0 and MLA_V == LANES
    ctx_blocks = n_ctx // tb

    def call(q_blocks, n_keys, q_maps, n_tiles, name):
        rows = q_blocks * tb
        whole = lambda i, t: (i, 0, 0)
        est = (4 * _nbytes((rows, w), BF16) + 2 * (2 * _nbytes((n_keys, w), BF16) + _nbytes((n_keys, LANES), BF16))
               + 2 * _nbytes((rows, w), BF16) + 3 * _nbytes((n_keys, rows), F32) + _nbytes((n_keys, 2 * LANES), BF16))
        return pl.pallas_call(
            functools.partial(_attn_kernel, q_blocks=q_blocks),
            grid=(b, n_tiles),
            in_specs=[pl.BlockSpec((1, tb, w), m) for m in q_maps] * 2
                     + [pl.BlockSpec((1, n_keys, w), whole), pl.BlockSpec((1, n_keys, LANES), whole),
                        pl.BlockSpec((1, w, n_keys), whole)],
            out_specs=pl.BlockSpec((1, rows, w), lambda i, t: (i, t, 0)),
            out_shape=jax.ShapeDtypeStruct((b, n_tiles * rows, w), BF16),
            scratch_shapes=[pltpu.VMEM((2, n_keys, rows), F32), pltpu.VMEM((n_keys, rows), BF16)],
            compiler_params=_params(est, 2),
            name=name,
        )(*([qn] * q_blocks), *([qp] * q_blocks), kn, kp, vt)

    att_c = call(1, n_ctx, [lambda i, t: (i, 0, 0)], 1, "attention_ctx")
    lat_maps = [lambda i, t, j=j: (i, ctx_blocks + lat_q_blocks * t + j, 0) for j in range(lat_q_blocks)]
    att_l = call(lat_q_blocks, s, lat_maps, (s - n_ctx) // tq, "attention")
    return att_c, att_l


def _pool_bands(tile_rows):
    t = np.arange(tile_rows)[:, None]
    j = np.arange(tile_rows + 2 * HALO)[None, :]
    bands = []
    for win in POOL_WINDOWS:
        lo = HALO + t - win // 2
        bands.append(((j >= lo) & (j < lo + win)).astype(np.float32))
    return np.stack(bands)


def _merge_kernel(h_ref, ml_ref, mc_ref, yf_ref, yb_ref, zs_ref, attc_ref, attl_ref,
                  pv_ref, pvp_ref, pvn_ref, cv_ref, cvp_ref, cvn_ref, sg_ref,
                  snw_ref, wso, wmo, band_ref, pw_ref, ps_ref, wpo, cw_ref, wco, wo,
                  o_ref, ext_u, *, n_ctx):
    tm = h_ref.shape[1]
    d = h_ref.shape[2]
    t = pl.program_id(1)
    s_len = pl.num_programs(1) * tm
    mod = _row_mods(ml_ref, mc_ref, t, tm, n_ctx)
    seq_start, seq_end = _seq_edges(t, tm, n_ctx, s_len)

    def gate(k):
        return sg_ref[0, :, k * d:(k + 1) * d].astype(F32)

    mla_b = _dot(jnp.where(t < n_ctx // tm, attc_ref[0], attl_ref[0]), wmo[...])
    merged = gate(1) * mla_b

    pv = pv_ref[0]
    ext_p = jnp.concatenate([jnp.where(seq_start, jnp.zeros_like(pvp_ref[0]), pvp_ref[0]), pv,
                             jnp.where(seq_end, jnp.zeros_like(pvn_ref[0]), pvn_ref[0])], axis=0)
    rows = t * tm + lax.broadcasted_iota(jnp.int32, (tm, 1), 0)
    in_ctx = rows < n_ctx
    pos = jnp.where(in_ctx, rows, rows - n_ctx)
    seq_n = jnp.where(in_ctx, n_ctx, s_len - n_ctx)
    pooled = []
    for gi, win in enumerate(POOL_WINDOWS):
        gc = slice(gi * POOL_GROUP, (gi + 1) * POOL_GROUP)
        left = win // 2
        lo = jnp.clip(pos - left, 0, seq_n)
        hi = jnp.clip(pos - left + win, 0, seq_n)
        p_g = _dot(band_ref[gi], ext_p[:, gc]) / (hi - lo).astype(F32) - pv[:, gc].astype(F32)
        pooled.append(_dot(p_g.astype(BF16), pw_ref[gi]))
    pool_y = jnp.concatenate(pooled, axis=1) * ps_ref[...]
    merged = merged + gate(2) * _dot(pool_y.astype(BF16), wpo[...])

    y = yf_ref[0].astype(F32) + yb_ref[0].astype(F32)
    ssd_in = _rms(y * zs_ref[0].astype(F32), snw_ref[...]).astype(BF16)
    merged = merged + gate(0) * _dot(ssd_in, wso[...])

    def gated_in(ref):
        return ref[0, :, CONV_WIDTH:2 * CONV_WIDTH].astype(F32) * ref[0, :, 2 * CONV_WIDTH:].astype(F32)
    ext_u[0:HALO, :] = jnp.where(seq_start, 0.0, gated_in(cvp_ref))
    ext_u[HALO:HALO + tm, :] = gated_in(cv_ref)
    ext_u[HALO + tm:, :] = jnp.where(seq_end, 0.0, gated_in(cvn_ref))
    conv = cw_ref[0:1, :] * ext_u[HALO - 1:HALO - 1 + tm, :]
    for k in range(1, CONV_K):
        conv = conv + cw_ref[k:k + 1, :] * ext_u[HALO - 1 + k:HALO - 1 + k + tm, :]
    conv_in = (cv_ref[0, :, :CONV_WIDTH].astype(F32) * conv).astype(BF16)
    merged = merged + gate(3) * _dot(conv_in, wco[...])
    o_ref[0] = h_ref[0] + mod(5) * _dot(merged.astype(BF16), wo[...])


def _merge(h, ml, mc, yf, yb, zs, att_c, att_l, pv, cv, sg, w, stacked, layer, *, n_ctx):
    b, s, d = h.shape
    tm = MIX_TILE
    assert s % tm == 0 and n_ctx % tm == 0 and tm % HALO == 0
    ctx_tiles = n_ctx // tm
    tile = lambda i, t: (i, t, 0)
    prev, nxt = _halo_maps(tm, s)
    bands = _pool_bands(tm)
    st = lambda name: (stacked[name], layer)
    weights = [(w["ssd_norm"], None), st("ssd_w_out"), st("mla_w_out"), (jnp.asarray(bands, BF16), None),
               st("pool_w"), (w["pool_scale"], None), st("pool_w_out"), (w["sconv_w"], None),
               st("sconv_w_out"), st("w_o")]
    w_specs = [_resident(x.shape) if l is None else _layer_resident(x, l) for x, l in weights]
    w_bytes = sum(_nbytes(x.shape if l is None else x.shape[1:], x.dtype) for x, l in weights)
    stream = (2 * _nbytes((tm, d), F32) + _nbytes((tm, 4 * SSD_INNER + POOL_WIDTH + 3 * CONV_WIDTH + N_BRANCH * d), BF16)
              + 2 * _nbytes((HALO, POOL_WIDTH + 3 * CONV_WIDTH), BF16))
    est = 2 * stream + w_bytes + _nbytes((tm + 2 * HALO, d), F32) + 10 * _nbytes((tm, d), F32)
    return pl.pallas_call(
        functools.partial(_merge_kernel, n_ctx=n_ctx),
        grid=(b, s // tm),
        in_specs=[pl.BlockSpec((1, tm, d), tile),
                  pl.BlockSpec((1, N_MOD, d), lambda i, t: (i, 0, 0)),
                  _resident(mc.shape),
                  pl.BlockSpec((1, tm, SSD_INNER), tile),
                  pl.BlockSpec((1, tm, SSD_INNER), tile),
                  pl.BlockSpec((1, tm, SSD_INNER), tile),
                  pl.BlockSpec((1, tm, MLA_HEADS * MLA_V), lambda i, t: (i, jnp.minimum(t, ctx_tiles - 1), 0)),
                  pl.BlockSpec((1, tm, MLA_HEADS * MLA_V), lambda i, t: (i, jnp.maximum(t - ctx_tiles, 0), 0)),
                  pl.BlockSpec((1, tm, POOL_WIDTH), tile),
                  pl.BlockSpec((1, HALO, POOL_WIDTH), prev),
                  pl.BlockSpec((1, HALO, POOL_WIDTH), nxt),
                  pl.BlockSpec((1, tm, 3 * CONV_WIDTH), tile),
                  pl.BlockSpec((1, HALO, 3 * CONV_WIDTH), prev),
                  pl.BlockSpec((1, HALO, 3 * CONV_WIDTH), nxt),
                  pl.BlockSpec((1, tm, N_BRANCH * d), tile)]
                 + w_specs,
        out_specs=pl.BlockSpec((1, tm, d), tile),
        out_shape=jax.ShapeDtypeStruct(h.shape, F32),
        scratch_shapes=[pltpu.VMEM((tm + 2 * HALO, CONV_WIDTH), F32)],
        compiler_params=_params(est, 2),
        name="merge",
    )(h, ml, mc, yf, yb, zs, att_c, att_l, pv, pv, pv, cv, cv, cv, sg, *[x for x, _ in weights])


def _rope_tables(n_ctx, n_lat):
    rows = n_lat // GRID_W
    row = np.repeat(np.arange(rows), GRID_W).astype(np.float32)
    col = np.tile(np.arange(GRID_W), rows).astype(np.float32)
    half = MLA_ROPE // 2
    inv = (1.0 / (ROPE_THETA ** (jnp.arange(0, half, 2, dtype=F32) / half)))
    ar = jnp.asarray(row)[:, None] * inv
    ac = jnp.asarray(col)[:, None] * inv
    ang = jnp.concatenate([ar, ar, ac, ac], axis=-1)
    pad = LANES - MLA_ROPE
    cos = jnp.concatenate([jnp.ones((n_ctx, MLA_ROPE), F32), jnp.cos(ang)], axis=0)
    sin = jnp.concatenate([jnp.zeros((n_ctx, MLA_ROPE), F32), jnp.sin(ang)], axis=0)
    return jnp.pad(cos, ((0, 0), (0, pad))), jnp.pad(sin, ((0, 0), (0, pad)))


def _rot_cols(w):
    w1, w2, w3, w4 = jnp.split(w, 4, axis=-1)
    return jnp.concatenate([-w2, w1, -w4, w3], axis=-1)


def _layer_weights(i, p):
    w_in = p["w_in"][i]
    splits = np.cumsum([SSD_INNER, SSD_XBC, 2 * SSD_HEADS, MLA_Q_RANK, MLA_KV_RANK, MLA_ROPE,
                        POOL_WIDTH, 3 * CONV_WIDTH])
    wz, wxbc, wdt, wqd, wkvd, wkr, wpv, wcv, wg = jnp.split(w_in, splits, axis=1)
    lane_pad = lambda x: jnp.pad(x, ((0, 0), (0, LANES - x.shape[1])))
    uq = p["mla_w_uq"][i].reshape(MLA_Q_RANK, MLA_HEADS, MLA_QK)
    uq_pe = uq[:, :, MLA_NOPE:]
    head_pad = lambda x: jnp.pad(x, ((0, 0), (0, 0), (0, LANES - MLA_ROPE))).reshape(MLA_Q_RANK, MLA_HEADS * LANES)
    bf = lambda x: x.astype(BF16)
    return {
        "wz": bf(wz), "wxbc": bf(wxbc),
        "wsm": bf(jnp.concatenate([wqd, wkvd, lane_pad(wkr), lane_pad(_rot_cols(wkr)), lane_pad(wdt)], axis=1)),
        "q_norm": p["mla_q_norm"][i].reshape(1, -1), "kv_norm": p["mla_kv_norm"][i].reshape(1, -1),
        "wuqn": bf(uq[:, :, :MLA_NOPE].reshape(MLA_Q_RANK, MLA_HEADS * MLA_NOPE)),
        "wuqp": bf(head_pad(uq_pe)), "wuqr": bf(head_pad(_rot_cols(uq_pe))),
        "wuk": bf(p["mla_w_uk"][i]), "wuvt": bf(p["mla_w_uv"][i].T),
        "wpv": bf(wpv), "wcv": bf(wcv), "wg": bf(wg),
        "ssd_norm": p["ssd_norm"][i].reshape(1, -1), "pool_scale": p["pool_scale"][i].reshape(1, -1),
        "sconv_w": p["sconv_w"][i],
    }


def kernel(x, c, ctx, c_ctx, w_ada, b_ada, ffn1_norm, ffn1_w_gate, ffn1_w_up, ffn1_w_down, mix_norm, w_in, ssd_conv_w, ssd_conv_b, ssd_dt_bias, ssd_a_log, ssd_d, ssd_norm, ssd_w_out, mla_q_norm, mla_w_uq, mla_kv_norm, mla_w_uk, mla_w_uv, mla_w_out, pool_w, pool_scale, pool_w_out, sconv_w, sconv_w_out, w_o, ffn2_norm, ffn2_w_gate, ffn2_w_up, ffn2_w_down, final_norm):
    p = dict(w_in=w_in.astype(BF16), ssd_norm=ssd_norm, mla_q_norm=mla_q_norm, mla_w_uq=mla_w_uq,
             mla_kv_norm=mla_kv_norm, mla_w_uk=mla_w_uk, mla_w_uv=mla_w_uv, pool_scale=pool_scale, sconv_w=sconv_w)
    stacked = {name: a.astype(BF16) for name, a in dict(
        ssd_w_out=ssd_w_out, mla_w_out=mla_w_out, pool_w=pool_w, pool_w_out=pool_w_out,
        sconv_w_out=sconv_w_out, w_o=w_o).items()}
    bsz, n_lat, d = x.shape
    n_ctx = ctx.shape[1]
    depth = w_ada.shape[0]

    mod_rows = -(-(bsz + 1) // 8) * 8
    cc = jnp.concatenate([c, c_ctx[None, :], jnp.zeros((mod_rows - bsz - 1, d), F32)], axis=0)
    mods = _adaln(cc, w_ada, b_ada).reshape(depth, mod_rows, N_MOD, d)

    cos_t, sin_t = _rope_tables(n_ctx, n_lat)
    ffn1_w = [a.astype(BF16) for a in (ffn1_w_gate, ffn1_w_up, ffn1_w_down)]
    ffn2_w = [a.astype(BF16) for a in (ffn2_w_gate, ffn2_w_up, ffn2_w_down)]
    h = x
    for i in range(depth):
        ml, mc = mods[i, :bsz], mods[i, bsz]
        w = _layer_weights(i, p)
        h = _ffn(h, ml, mc, ffn1_norm[i], *ffn1_w, i, n_ctx=n_ctx, mod0=0, ctx=ctx if i == 0 else None)
        zs, xs, cm, bt, dt, qn, qp, kn, kp, vt = _proj_seq(
            h, ml, mc, mix_norm[i], cos_t, sin_t, ssd_conv_w[i], ssd_conv_b[i], w, n_ctx=n_ctx)
        pv, cv, sg = _proj_loc(h, ml, mc, mix_norm[i], w, n_ctx=n_ctx)
        yf, yb = _ssd_scan(xs, cm, bt, dt, ssd_dt_bias[i], ssd_a_log[i], ssd_d[i], n_ctx=n_ctx)
        att_c, att_l = _attention(qn, qp, kn, kp, vt, n_ctx=n_ctx)
        h = _merge(h, ml, mc, yf, yb, zs, att_c, att_l, pv, cv, sg, w, stacked, i, n_ctx=n_ctx)
        h = _ffn(h, ml, mc, ffn2_norm[i], *ffn2_w, i, n_ctx=n_ctx, mod0=6,
                 final_w=final_norm if i == depth - 1 else None)
    return h
```

```python
import functools
import math

import jax
import jax.numpy as jnp
import numpy as np
from jax import lax
from jax.experimental import pallas as pl
from jax.experimental.pallas import tpu as pltpu

F32 = jnp.float32
BF16 = jnp.bfloat16

EPS = 1e-6
LOG2E = math.log2(math.e)
FFN_RES = 0.5
N_MOD = 9
GRID_W = 64
ROPE_THETA = 10000.0

SSD_HEADS = 16
SSD_HEAD_DIM = 64
SSD_GROUPS = 4
SSD_STATE = 128
SSD_CONV = 4
SSD_CHUNK = 128
SSD_INNER = SSD_HEADS * SSD_HEAD_DIM
SSD_BC = SSD_GROUPS * SSD_STATE
SSD_XBC = SSD_INNER + 2 * SSD_BC
HEADS_PER_GROUP = SSD_HEADS // SSD_GROUPS

MLA_HEADS = 8
MLA_Q_RANK = 384
MLA_KV_RANK = 256
MLA_NOPE = 128
MLA_ROPE = 64
MLA_V = 128
MLA_QK = MLA_NOPE + MLA_ROPE

POOL_WINDOWS = (2, 4, 8, 16)
POOL_GROUP = 256
POOL_WIDTH = POOL_GROUP * len(POOL_WINDOWS)
CONV_WIDTH = 1024
CONV_K = 3
N_BRANCH = 4

LANES = 128
BF16_ROWS = 16
HALO = BF16_ROWS
VMEM_CAP = 56 * 1024 * 1024
VMEM_SLACK = 12 * 1024 * 1024

ROW_TILE = 768
MIX_TILE = 256
FFN_CHUNK = 256
ATTN_KEY_BLOCK = 256


def _params(est_bytes, n_axes):
    return pltpu.CompilerParams(
        dimension_semantics=("arbitrary",) * n_axes,
        vmem_limit_bytes=int(min(est_bytes + VMEM_SLACK, VMEM_CAP)))


def _resident(shape):
    nd = len(shape)
    return pl.BlockSpec(shape, lambda *_: (0,) * nd, pipeline_mode=pl.Buffered(1))


def _layer_resident(x, layer):
    nd = x.ndim - 1
    return pl.BlockSpec((None,) + x.shape[1:], lambda *_: (layer,) + (0,) * nd, pipeline_mode=pl.Buffered(1))


def _nbytes(shape, dtype):
    return int(np.prod(shape)) * jnp.dtype(dtype).itemsize


def _dot(a, b):
    return jnp.dot(a, b, preferred_element_type=F32)


def _dot_nt(a, b):
    return lax.dot_general(a, b, (((1,), (1,)), ((), ())), preferred_element_type=F32)


def _silu(x):
    return x * jax.nn.sigmoid(x)


def _rms(x, w):
    ms = jnp.mean(x * x, axis=-1, keepdims=True)
    return x * lax.rsqrt(ms + EPS) * w


def _row_mods(ml_ref, mc_ref, tile, tile_rows, n_ctx):
    rows = tile * tile_rows + lax.broadcasted_iota(jnp.int32, (tile_rows, 1), 0)
    is_ctx = rows < n_ctx

    def mod(k):
        return jnp.where(is_ctx, mc_ref[k:k + 1, :], ml_ref[0, k:k + 1, :])
    return mod


def _halo_maps(tile_rows, s_len):
    hb = tile_rows // HALO
    last = s_len // HALO - 1
    prev = lambda i, t: (i, jnp.maximum(t * hb - 1, 0), 0)
    nxt = lambda i, t: (i, jnp.minimum((t + 1) * hb, last), 0)
    return prev, nxt


def _seq_edges(tile, tile_rows, n_ctx, s_len):
    row0 = tile * tile_rows
    start = jnp.logical_or(row0 == 0, row0 == n_ctx)
    end = jnp.logical_or(row0 + tile_rows == n_ctx, row0 + tile_rows == s_len)
    return start, end


def _ada_kernel(c_ref, w_ref, b_ref, o_ref):
    s = _silu(c_ref[...]).astype(BF16)
    o_ref[0] = _dot(s, w_ref[0].astype(BF16)) + b_ref[0]


def _adaln(cc, w_ada, b_ada):
    depth, d, n = w_ada.shape
    rows = cc.shape[0]
    tn = n // 8
    est = 2 * (_nbytes((d, tn), F32) + _nbytes((rows, tn), F32)) + _nbytes((d, tn), BF16)
    return pl.pallas_call(
        _ada_kernel,
        grid=(depth, n // tn),
        in_specs=[pl.BlockSpec((rows, d), lambda i, j: (0, 0)),
                  pl.BlockSpec((1, d, tn), lambda i, j: (i, 0, j)),
                  pl.BlockSpec((1, 1, tn), lambda i, j: (i, 0, j))],
        out_specs=pl.BlockSpec((1, rows, tn), lambda i, j: (i, 0, j)),
        out_shape=jax.ShapeDtypeStruct((depth, rows, n), F32),
        compiler_params=_params(est, 2),
        name="adaln",
    )(cc, w_ada, b_ada.reshape(depth, 1, n))


def _ffn_body(h, mod, mod0, nw_ref, wg_ref, wu_ref, wd_ref, p_scr):
    d_ff = wg_ref.shape[1]
    a = (_rms(h, nw_ref[...]) * (1.0 + mod(mod0 + 1)) + mod(mod0)).astype(BF16)
    for j in range(d_ff // FFN_CHUNK):
        cols = slice(j * FFN_CHUNK, (j + 1) * FFN_CHUNK)
        g = _dot(a, wg_ref[:, cols])
        u = _dot(a, wu_ref[:, cols])
        p_scr[:, cols] = (_silu(g) * u).astype(BF16)
    y = _dot(p_scr[...], wd_ref[...])
    return h + (FFN_RES * mod(mod0 + 2)) * y


def _ffn_kernel(h_ref, ml_ref, mc_ref, nw_ref, wg_ref, wu_ref, wd_ref, o_ref, p_scr, *, n_ctx, mod0):
    tm = h_ref.shape[1]
    mod = _row_mods(ml_ref, mc_ref, pl.program_id(1), tm, n_ctx)
    o_ref[0] = _ffn_body(h_ref[0], mod, mod0, nw_ref, wg_ref, wu_ref, wd_ref, p_scr)


def _ffn_first_kernel(ctx_ref, x_ref, ml_ref, mc_ref, nw_ref, wg_ref, wu_ref, wd_ref, o_ref, p_scr, *, n_ctx, mod0):
    tm = x_ref.shape[1]
    t = pl.program_id(1)
    mod = _row_mods(ml_ref, mc_ref, t, tm, n_ctx)
    h = jnp.where(t < n_ctx // tm, ctx_ref[0], x_ref[0])
    o_ref[0] = _ffn_body(h, mod, mod0, nw_ref, wg_ref, wu_ref, wd_ref, p_scr)


def _ffn_last_kernel(h_ref, ml_ref, mc_ref, nw_ref, wg_ref, wu_ref, wd_ref, fw_ref, o_ref, p_scr, *, n_ctx, mod0):
    tm = h_ref.shape[1]
    mod = _row_mods(ml_ref, mc_ref, pl.program_id(1), tm, n_ctx)
    o_ref[0] = _rms(_ffn_body(h_ref[0], mod, mod0, nw_ref, wg_ref, wu_ref, wd_ref, p_scr), fw_ref[...])


def _ffn(h, ml, mc, norm_w, wg, wu, wd, layer, *, n_ctx, mod0, ctx=None, final_w=None):
    b, _, d = h.shape
    d_ff = wg.shape[2]
    layer_block = lambda x: _layer_resident(x, layer)
    special = ctx is not None or final_w is not None
    tm = MIX_TILE if special else ROW_TILE
    s = h.shape[1] + (ctx.shape[1] if ctx is not None else 0)
    assert s % tm == 0 and d_ff % FFN_CHUNK == 0 and (not special or n_ctx % tm == 0)
    ctx_tiles = n_ctx // tm
    est = (4 * _nbytes((tm, d), F32) + 3 * _nbytes((d, d_ff), BF16) + _nbytes((tm, d_ff), BF16)
           + 4 * _nbytes((tm, FFN_CHUNK), F32) + 2 * _nbytes((tm, d), F32))
    tile = lambda i, t: (i, t, 0)
    lat_tile = lambda i, t: (i, jnp.maximum(t - ctx_tiles, 0), 0)
    common = [pl.BlockSpec((1, N_MOD, d), lambda i, t: (i, 0, 0)),
              _resident(mc.shape), _resident((1, d)),
              layer_block(wg), layer_block(wu), layer_block(wd)]
    args = [ml, mc, norm_w.reshape(1, d), wg, wu, wd]
    out_rows, out_spec = s, pl.BlockSpec((1, tm, d), tile)
    if ctx is not None:
        body = _ffn_first_kernel
        in_specs = [pl.BlockSpec((1, tm, d), lambda i, t: (i, jnp.minimum(t, ctx_tiles - 1), 0)),
                    pl.BlockSpec((1, tm, d), lat_tile)] + common
        args = [ctx, h] + args
    elif final_w is not None:
        body = _ffn_last_kernel
        in_specs = [pl.BlockSpec((1, tm, d), tile)] + common + [_resident((1, d))]
        args = [h] + args + [final_w.reshape(1, d)]
        out_rows, out_spec = s - n_ctx, pl.BlockSpec((1, tm, d), lat_tile)
    else:
        body = _ffn_kernel
        in_specs = [pl.BlockSpec((1, tm, d), tile)] + common
        args = [h] + args
    return pl.pallas_call(
        functools.partial(body, n_ctx=n_ctx, mod0=mod0),
        grid=(b, s // tm),
        in_specs=in_specs,
        out_specs=out_spec,
        out_shape=jax.ShapeDtypeStruct((b, out_rows, d), F32),
        scratch_shapes=[pltpu.VMEM((tm, d_ff), BF16)],
        compiler_params=_params(est, 2),
        name="ffn",
    )(*args)


def _proj_seq_kernel(h_ref, hp_ref, hn_ref, ml_ref, mc_ref, nw_ref, cos_ref, sin_ref, cw_ref, cb_ref,
                     wzx, wsm, qnw_ref, kvnw_ref, wuqn, wuqp, wuqr, wuk, wuvt,
                     zs_o, xs_o, c_o, bt_o, dt_o, qn_o, qp_o, kn_o, kp_o, vt_o, *, n_ctx, qk_scale):
    tm = h_ref.shape[1]
    t = pl.program_id(1)
    s_len = pl.num_programs(1) * tm
    row0 = t * tm

    def normed(ref, first_row):
        rows = first_row + lax.broadcasted_iota(jnp.int32, (ref.shape[1], 1), 0)
        is_ctx = rows < n_ctx
        mod = lambda k: jnp.where(is_ctx, mc_ref[k:k + 1, :], ml_ref[0, k:k + 1, :])
        return (_rms(ref[0], nw_ref[...]) * (1.0 + mod(4)) + mod(3)).astype(BF16)

    a = normed(h_ref, row0)
    a_ext = jnp.concatenate([normed(hp_ref, row0 - HALO), a, normed(hn_ref, row0 + tm)], axis=0)
    xbc = _dot(a_ext, wzx[:, SSD_INNER:])
    zs_o[0] = _silu(_dot(a, wzx[:, :SSD_INNER])).astype(BF16)
    sm = _dot(a, wsm[...])
    c0 = MLA_Q_RANK + MLA_KV_RANK
    qd, kvd, kr, kr_rot = sm[:, :MLA_Q_RANK], sm[:, MLA_Q_RANK:c0], sm[:, c0:c0 + LANES], sm[:, c0 + LANES:c0 + 2 * LANES]
    dt_o[0] = sm[:, c0 + 2 * LANES:]
    cos = cos_ref[...]
    sin = sin_ref[...]
    cos_h = jnp.concatenate([cos] * MLA_HEADS, axis=1)
    sin_h = jnp.concatenate([sin] * MLA_HEADS, axis=1)
    qn = _rms(qd, qnw_ref[...]).astype(BF16)
    qn_o[0] = (_dot(qn, wuqn[...]) * qk_scale).astype(BF16)
    qp_o[0] = ((_dot(qn, wuqp[...]) * cos_h + _dot(qn, wuqr[...]) * sin_h) * qk_scale).astype(BF16)
    ckv = _rms(kvd, kvnw_ref[...]).astype(BF16)
    kn_o[0] = _dot(ckv, wuk[...]).astype(BF16)
    vt_o[0] = _dot_nt(wuvt[...], ckv).astype(BF16)
    kp_o[0] = (kr * cos + kr_rot * sin).astype(BF16)

    ext_rows = tm + 2 * HALO
    pad_l = SSD_CONV // 2
    rows = row0 + lax.broadcasted_iota(jnp.int32, (tm, LANES), 0)
    in_ctx = rows < n_ctx
    pos = jnp.where(in_ctx, rows, rows - n_ctx)
    seq_n = jnp.where(in_ctx, n_ctx, s_len - n_ctx)
    valid = [pos + (k - pad_l) >= 0 if k < pad_l else pos + (k - pad_l) < seq_n for k in range(SSD_CONV)]
    sub = 8
    starts = sorted({0, n_ctx % tm // sub * sub})
    ends = sorted({(n_ctx - 1) % tm // sub * sub, (tm - 1) // sub * sub})

    def masked(tap, k):
        pieces, r = [], 0
        for g in (starts if k < pad_l else ends):
            pieces += [tap[r:g], jnp.where(valid[k][g:g + sub], tap[g:g + sub], 0.0)]
            r = g + sub
        return jnp.concatenate([p for p in pieces + [tap[r:]] if p.shape[0]], axis=0)

    q, n = SSD_CHUNK, SSD_STATE
    for j in range(SSD_XBC // LANES):
        lc = slice(j * LANES, (j + 1) * LANES)
        ext = xbc[:, lc]
        acc = cb_ref[:, lc] + cw_ref[pad_l:pad_l + 1, lc] * ext[HALO:HALO + tm]
        for k in range(SSD_CONV):
            if k != pad_l:
                tap = pltpu.roll(ext, (pad_l - k) % ext_rows, axis=0)[HALO:HALO + tm]
                acc = acc + cw_ref[k:k + 1, lc] * masked(tap, k)
        act = _silu(acc)
        col = j * LANES
        if col < SSD_INNER:
            xs_o[0, :, lc] = act.astype(BF16)
        elif col < SSD_INNER + SSD_BC:
            g = (col - SSD_INNER) // n
            for ci in range(tm // q):
                r0 = (ci * SSD_GROUPS + g) * n
                bt_o[0, r0:r0 + n, :] = act[ci * q:(ci + 1) * q, :].T.astype(BF16)
        else:
            cc = col - SSD_INNER - SSD_BC
            c_o[0, :, cc:cc + LANES] = act.astype(BF16)


def _proj_seq(h, ml, mc, norm_w, cos_t, sin_t, conv_w, conv_b, w, *, n_ctx):
    b, s, d = h.shape
    tm = ROW_TILE
    q = SSD_CHUNK
    assert s % tm == 0 and tm % q == 0 and tm % HALO == 0 and n_ctx % HALO == 0 and SSD_STATE == LANES
    assert MLA_Q_RANK % LANES == 0 and MLA_KV_RANK % LANES == 0
    weights = [w["wzx"], w["wsm"], w["q_norm"], w["kv_norm"],
               w["wuqn"], w["wuqp"], w["wuqr"], w["wuk"], w["wuvt"]]
    widths = [(SSD_INNER, BF16), (SSD_INNER, BF16), (SSD_BC, BF16), None, (LANES, F32),
              (MLA_HEADS * MLA_NOPE, BF16), (MLA_HEADS * LANES, BF16), (MLA_HEADS * MLA_NOPE, BF16), (LANES, BF16)]
    tile = lambda i, t: (i, t, 0)
    prev, nxt = _halo_maps(tm, s)
    vw = MLA_HEADS * MLA_V
    bt_rows = tm // q * SSD_BC
    row_outs = [x for x in widths if x is not None]
    est = (2 * _nbytes((tm + 2 * HALO, d), F32) + sum(_nbytes(x.shape, x.dtype) for x in weights)
           + 2 * sum(_nbytes((tm, n), dt) for n, dt in row_outs) + 2 * _nbytes((vw, tm), BF16)
           + 2 * _nbytes((bt_rows, q), BF16) + 4 * _nbytes((tm + 2 * HALO, SSD_XBC), F32))
    out_specs, out_shape = [], []
    for x in widths:
        if x is None:
            out_specs.append(pl.BlockSpec((1, bt_rows, q), tile))
            out_shape.append(jax.ShapeDtypeStruct((b, s // q * SSD_BC, q), BF16))
        else:
            out_specs.append(pl.BlockSpec((1, tm, x[0]), tile))
            out_shape.append(jax.ShapeDtypeStruct((b, s, x[0]), x[1]))
    return pl.pallas_call(
        functools.partial(_proj_seq_kernel, n_ctx=n_ctx, qk_scale=MLA_QK ** -0.5 * LOG2E),
        grid=(b, s // tm),
        in_specs=[pl.BlockSpec((1, tm, d), tile),
                  pl.BlockSpec((1, HALO, d), prev),
                  pl.BlockSpec((1, HALO, d), nxt),
                  pl.BlockSpec((1, N_MOD, d), lambda i, t: (i, 0, 0)),
                  _resident(mc.shape), _resident((1, d)),
                  pl.BlockSpec((tm, LANES), lambda i, t: (t, 0)),
                  pl.BlockSpec((tm, LANES), lambda i, t: (t, 0)),
                  _resident(conv_w.shape), _resident((1, SSD_XBC))]
                 + [_resident(x.shape) for x in weights],
        out_specs=out_specs + [pl.BlockSpec((1, vw, tm), lambda i, t: (i, 0, t))],
        out_shape=out_shape + [jax.ShapeDtypeStruct((b, vw, s), BF16)],
        compiler_params=_params(est, 2),
        name="proj_seq",
    )(h, h, h, ml, mc, norm_w.reshape(1, d), cos_t, sin_t, conv_w, conv_b.reshape(1, -1), *weights)


def _proj_loc_kernel(h_ref, ml_ref, mc_ref, nw_ref, wloc, pv_o, cv_o, g_o, *, n_ctx):
    tm = h_ref.shape[1]
    mod = _row_mods(ml_ref, mc_ref, pl.program_id(1), tm, n_ctx)
    a = (_rms(h_ref[0], nw_ref[...]) * (1.0 + mod(4)) + mod(3)).astype(BF16)
    c1 = POOL_WIDTH + 3 * CONV_WIDTH
    pv_o[0] = _dot(a, wloc[:, :POOL_WIDTH]).astype(BF16)
    cv_o[0] = _dot(a, wloc[:, POOL_WIDTH:c1]).astype(BF16)
    g_o[0] = jax.nn.sigmoid(_dot(a, wloc[:, c1:])).astype(BF16)


def _proj_loc(h, ml, mc, norm_w, w, *, n_ctx):
    b, s, d = h.shape
    tm = ROW_TILE
    weights = [w["wloc"]]
    widths = [POOL_WIDTH, 3 * CONV_WIDTH, N_BRANCH * d]
    assert sum(widths) == w["wloc"].shape[1]
    tile = lambda i, t: (i, t, 0)
    est = (2 * _nbytes((tm, d), F32) + sum(_nbytes(x.shape, BF16) for x in weights)
           + 2 * sum(_nbytes((tm, n), BF16) for n in widths) + 2 * _nbytes((tm, max(widths)), F32))
    return pl.pallas_call(
        functools.partial(_proj_loc_kernel, n_ctx=n_ctx),
        grid=(b, s // tm),
        in_specs=[pl.BlockSpec((1, tm, d), tile),
                  pl.BlockSpec((1, N_MOD, d), lambda i, t: (i, 0, 0)),
                  _resident(mc.shape), _resident((1, d))]
                 + [_resident(x.shape) for x in weights],
        out_specs=[pl.BlockSpec((1, tm, n), tile) for n in widths],
        out_shape=[jax.ShapeDtypeStruct((b, s, n), BF16) for n in widths],
        compiler_params=_params(est, 2),
        name="proj_loc",
    )(h, ml, mc, norm_w.reshape(1, d), *weights)


def _split_bf16(x, parts):
    out = []
    for _ in range(parts):
        p = x.astype(BF16)
        out.append(p)
        x = x - p.astype(F32)
    return out


class _SsdChunk:
    def __init__(self, d, sub, xs_ref, c_ref, bt_ref, dt_ref, y_ref, st_ref, bias_ref, alog_ref, dskip_ref,
                 expand_ref, tri_ref):
        self.d = d
        self.rows = slice(sub * SSD_CHUNK, (sub + 1) * SSD_CHUNK)
        self.bt_row0 = sub * SSD_BC
        self.xs_ref, self.c_ref, self.bt_ref, self.dt_ref = xs_ref, c_ref, bt_ref, dt_ref
        self.y_ref, self.st_ref = y_ref, st_ref
        self.bias_ref, self.alog_ref, self.dskip_ref = bias_ref, alog_ref, dskip_ref
        self.expand_ref, self.tri_ref = expand_ref, tri_ref

    def cumsum(self):
        x = self.dt_ref[0, self.rows, :] + self.bias_ref[...]
        self.dtv = jnp.maximum(x, 0.0) + jnp.log1p(jnp.exp(-jnp.abs(x)))
        da = self.dtv * (-LOG2E * jnp.exp(self.alog_ref[...]))
        tri = self.tri_ref[self.d]
        self.cs = sum(_dot(tri, p) for p in _split_bf16(da, 3))

    def expand(self):
        q = SSD_CHUNK
        last = q - 1 if self.d == 0 else 0
        cs, dtv = self.cs, self.dtv
        tot = cs[last:last + 1, :]
        self.src_t = (cs - jnp.log2(dtv)).T
        stack = jnp.concatenate(
            [dtv * jnp.exp2(tot - cs), jnp.exp2(cs), jnp.broadcast_to(jnp.exp2(tot), (BF16_ROWS, LANES))], axis=0)
        ex = _dot(stack.astype(BF16), self.expand_ref[self.d])
        self.ecs_exp = ex[q:2 * q]
        self.etot_exp = ex[2 * q:2 * q + 1]
        self.xs = self.xs_ref[0, self.rows, :]
        self.xe = (self.xs.astype(F32) * ex[:q]).astype(BF16)

    def group(self, g):
        d, q, n = self.d, SSD_CHUNK, SSD_STATE
        gw = HEADS_PER_GROUP * SSD_HEAD_DIM
        li = lax.broadcasted_iota(jnp.int32, (q, q), 0)
        si = lax.broadcasted_iota(jnp.int32, (q, q), 1)
        causal = (li >= si) if d == 0 else (li <= si)
        head_of_lane = lax.broadcasted_iota(jnp.int32, (q, gw), 1) // SSD_HEAD_DIM
        gc = slice(g * gw, (g + 1) * gw)
        c_g = self.c_ref[0, self.rows, g * n:(g + 1) * n]
        bt_g = self.bt_ref[0, self.bt_row0 + g * n:self.bt_row0 + (g + 1) * n, :]
        cb = _dot(c_g, bt_g)
        st_g = self.st_ref[:, gc]
        y_off = _dot(c_g, st_g.astype(BF16))
        xs_g = self.xs[:, gc]
        m_parts, x_parts = [], []
        for r in range(HEADS_PER_GROUP):
            col = d * SSD_HEADS + g * HEADS_PER_GROUP + r
            seg = self.cs[:, col:col + 1] - self.src_t[col:col + 1, :]
            m_parts.append((cb * jnp.exp2(jnp.where(causal, seg, -jnp.inf))).astype(BF16))
            x_parts.append(jnp.where(head_of_lane == r, xs_g, jnp.zeros_like(xs_g)))
        y_g = _dot(jnp.concatenate(m_parts, axis=1), jnp.concatenate(x_parts, axis=0))
        y_g = y_g + y_off * self.ecs_exp[:, gc]
        if d == 0:
            y_g = y_g + xs_g.astype(F32) * self.dskip_ref[:, gc]
        self.y_ref[0, self.rows, gc] = y_g.astype(BF16)
        self.st_ref[:, gc] = st_g * self.etot_exp[:, gc] + _dot(bt_g, self.xe[:, gc])


SCAN_CHUNKS = 2


def _ssd_scan_kernel(xs_f, c_f, bt_f, dt_f, xs_b, c_b, bt_b, dt_b, bias_ref, alog_ref, dskip_ref, expand_ref,
                     tri_ref, yf_ref, yb_ref, stf_s, stb_s):
    @pl.when(pl.program_id(1) == 0)
    def _reset():
        stf_s[...] = jnp.zeros(stf_s.shape, F32)
        stb_s[...] = jnp.zeros(stb_s.shape, F32)
    shared = (bias_ref, alog_ref, dskip_ref, expand_ref, tri_ref)
    waves = [(_SsdChunk(0, k, xs_f, c_f, bt_f, dt_f, yf_ref, stf_s, *shared),
              _SsdChunk(1, SCAN_CHUNKS - 1 - k, xs_b, c_b, bt_b, dt_b, yb_ref, stb_s, *shared))
             for k in range(SCAN_CHUNKS)]
    for wave in waves:
        for ch in wave:
            ch.cumsum()
    for wave in waves:
        for ch in wave:
            ch.expand()
    for wave in waves:
        for g in range(SSD_GROUPS):
            for ch in wave:
                ch.group(g)


def _ssd_scan(xs, cm, bt, dt, dt_bias, a_log, d_skip, *, n_ctx):
    b, s, _ = xs.shape
    q = SSD_CHUNK
    rows = SCAN_CHUNKS * q
    assert s % rows == 0 and n_ctx % rows == 0
    nb = s // rows
    nbt = n_ctx // rows
    pad = LANES - 2 * SSD_HEADS
    bias_row = jnp.pad(dt_bias.reshape(1, -1), ((0, 0), (0, pad)))
    alog_row = jnp.pad(a_log.reshape(1, -1), ((0, 0), (0, pad)))
    dskip_row = jnp.repeat(d_skip, SSD_HEAD_DIM).reshape(1, SSD_INNER)
    e = np.zeros((2, LANES, SSD_INNER), np.float32)
    for d in range(2):
        for h in range(SSD_HEADS):
            e[d, d * SSD_HEADS + h, h * SSD_HEAD_DIM:(h + 1) * SSD_HEAD_DIM] = 1.0
    low = np.tril(np.ones((q, q), np.float32))
    tri = np.stack([low, low.T])
    fwd = lambda i, c: (i, c, 0)
    bwd = lambda i, c: (i, jnp.where(c < nbt, nbt - 1 - c, nb - 1 - c + nbt), 0)
    per_dir = lambda m: [pl.BlockSpec((1, rows, SSD_INNER), m), pl.BlockSpec((1, rows, SSD_BC), m),
                         pl.BlockSpec((1, SCAN_CHUNKS * SSD_BC, q), m), pl.BlockSpec((1, rows, LANES), m)]
    est = (4 * (_nbytes((rows, SSD_INNER + SSD_BC), BF16) + _nbytes((SCAN_CHUNKS * SSD_BC, q), BF16)
                + _nbytes((rows, LANES), F32))
           + 4 * _nbytes((rows, SSD_INNER), BF16) + 2 * _nbytes((SSD_STATE, SSD_INNER), F32)
           + _nbytes(e.shape, BF16) + 12 * SCAN_CHUNKS * _nbytes((q, SSD_INNER), F32))
    return pl.pallas_call(
        _ssd_scan_kernel,
        grid=(b, nb),
        in_specs=per_dir(fwd) + per_dir(bwd)
                 + [_resident((1, LANES)), _resident((1, LANES)), _resident((1, SSD_INNER)),
                    _resident(e.shape), _resident(tri.shape)],
        out_specs=[pl.BlockSpec((1, rows, SSD_INNER), fwd), pl.BlockSpec((1, rows, SSD_INNER), bwd)],
        out_shape=[jax.ShapeDtypeStruct((b, s, SSD_INNER), BF16)] * 2,
        scratch_shapes=[pltpu.VMEM((SSD_STATE, SSD_INNER), F32)] * 2,
        compiler_params=_params(est, 2),
        name="ssd_scan",
    )(xs, cm, bt, dt, xs, cm, bt, dt, bias_row, alog_row, dskip_row, jnp.asarray(e, BF16), jnp.asarray(tri, BF16))


def _attn_kernel(*refs, q_blocks):
    qn_refs, qp_refs = refs[:q_blocks], refs[q_blocks:2 * q_blocks]
    kn_ref, kp_ref, vt_ref, o_ref, st_scr, p_scr = refs[2 * q_blocks:]
    kp = kp_ref[0]
    n_keys, n_q = p_scr.shape
    kb = ATTN_KEY_BLOCK
    sub = 8

    def scores(h):
        hc = slice(h * LANES, (h + 1) * LANES)
        qc = jnp.concatenate(
            [jnp.concatenate([qn[0, :, hc], qp[0, :, hc]], axis=1) for qn, qp in zip(qn_refs, qp_refs)], axis=0)
        kc = jnp.concatenate([kn_ref[0, :, hc], kp], axis=1)
        st_scr[h % 2] = _dot_nt(kc, qc)

    scores(0)
    for h in range(MLA_HEADS):
        if h + 1 < MLA_HEADS:
            scores(h + 1)
        st = st_scr.at[h % 2]
        hc = slice(h * LANES, (h + 1) * LANES)
        blocks = [slice(c * kb, (c + 1) * kb) for c in range(n_keys // kb)]
        m8 = None
        for rows in blocks:
            bm = jnp.max(st[rows, :].reshape(kb // sub, sub, n_q), axis=0)
            m8 = bm if m8 is None else jnp.maximum(m8, bm)
        m = jnp.max(m8, axis=0, keepdims=True)
        l8 = jnp.zeros((sub, n_q), F32)
        for rows in blocks:
            p = jnp.exp2(st[rows, :] - m)
            l8 = l8 + jnp.sum(p.reshape(kb // sub, sub, n_q), axis=0)
            p_scr[rows, :] = p.astype(BF16)
        denom = jnp.sum(l8, axis=0, keepdims=True)
        ot = _dot(vt_ref[0, hc, :], p_scr[...])
        o_ref[0, :, hc] = (ot / denom).T.astype(BF16)


def _attention(qn, qp, kn, kp, vt, *, n_ctx):
    b, s, w = qn.shape
    tb = MIX_TILE
    lat_q_blocks = 2
    tq = tb * lat_q_blocks
    assert n_ctx == tb and (s - n_ctx) % tq == 0 and MLA_V == LANES
    ctx_blocks = n_ctx // tb

    def call(q_blocks, n_keys, q_maps, n_tiles, name):
        rows = q_blocks * tb
        whole = lambda i, t: (i, 0, 0)
        est = (4 * _nbytes((rows, w), BF16) + 2 * (2 * _nbytes((n_keys, w), BF16) + _nbytes((n_keys, LANES), BF16))
               + 2 * _nbytes((rows, w), BF16) + 3 * _nbytes((n_keys, rows), F32) + _nbytes((n_keys, 2 * LANES), BF16))
        return pl.pallas_call(
            functools.partial(_attn_kernel, q_blocks=q_blocks),
            grid=(b, n_tiles),
            in_specs=[pl.BlockSpec((1, tb, w), m) for m in q_maps] * 2
                     + [pl.BlockSpec((1, n_keys, w), whole), pl.BlockSpec((1, n_keys, LANES), whole),
                        pl.BlockSpec((1, w, n_keys), whole)],
            out_specs=pl.BlockSpec((1, rows, w), lambda i, t: (i, t, 0)),
            out_shape=jax.ShapeDtypeStruct((b, n_tiles * rows, w), BF16),
            scratch_shapes=[pltpu.VMEM((2, n_keys, rows), F32), pltpu.VMEM((n_keys, rows), BF16)],
            compiler_params=_params(est, 2),
            name=name,
        )(*([qn] * q_blocks), *([qp] * q_blocks), kn, kp, vt)

    att_c = call(1, n_ctx, [lambda i, t: (i, 0, 0)], 1, "attention_ctx")
    lat_maps = [lambda i, t, j=j: (i, ctx_blocks + lat_q_blocks * t + j, 0) for j in range(lat_q_blocks)]
    att_l = call(lat_q_blocks, s, lat_maps, (s - n_ctx) // tq, "attention")
    return att_c, att_l


def _pool_bands(tile_rows):
    t = np.arange(tile_rows)[:, None]
    j = np.arange(tile_rows + 2 * HALO)[None, :]
    bands = []
    for win in POOL_WINDOWS:
        lo = HALO + t - win // 2
        bands.append(((j >= lo) & (j < lo + win)).astype(np.float32))
    return np.stack(bands)


def _merge_kernel(h_ref, ml_ref, mc_ref, yf_ref, yb_ref, zs_ref, attc_ref, attl_ref,
                  pv_ref, pvp_ref, pvn_ref, cv_ref, cvp_ref, cvn_ref, sg_ref,
                  snw_ref, wso, wmo, band_ref, pw_ref, ps_ref, wpo, cw_ref, wco, wo,
                  o_ref, ext_u, *, n_ctx):
    tm = h_ref.shape[1]
    d = h_ref.shape[2]
    t = pl.program_id(1)
    s_len = pl.num_programs(1) * tm
    mod = _row_mods(ml_ref, mc_ref, t, tm, n_ctx)
    seq_start, seq_end = _seq_edges(t, tm, n_ctx, s_len)

    def gate(k):
        return sg_ref[0, :, k * d:(k + 1) * d].astype(F32)

    mla_b = _dot(jnp.where(t < n_ctx // tm, attc_ref[0], attl_ref[0]), wmo[...])
    merged = gate(1) * mla_b

    pv = pv_ref[0]
    ext_p = jnp.concatenate([jnp.where(seq_start, jnp.zeros_like(pvp_ref[0]), pvp_ref[0]), pv,
                             jnp.where(seq_end, jnp.zeros_like(pvn_ref[0]), pvn_ref[0])], axis=0)
    rows = t * tm + lax.broadcasted_iota(jnp.int32, (tm, 1), 0)
    in_ctx = rows < n_ctx
    pos = jnp.where(in_ctx, rows, rows - n_ctx)
    seq_n = jnp.where(in_ctx, n_ctx, s_len - n_ctx)
    pooled = []
    for gi, win in enumerate(POOL_WINDOWS):
        gc = slice(gi * POOL_GROUP, (gi + 1) * POOL_GROUP)
        left = win // 2
        lo = jnp.clip(pos - left, 0, seq_n)
        hi = jnp.clip(pos - left + win, 0, seq_n)
        p_g = _dot(band_ref[gi], ext_p[:, gc]) / (hi - lo).astype(F32) - pv[:, gc].astype(F32)
        pooled.append(_dot(p_g.astype(BF16), pw_ref[gi]))
    pool_y = jnp.concatenate(pooled, axis=1) * ps_ref[...]
    merged = merged + gate(2) * _dot(pool_y.astype(BF16), wpo[...])

    y = yf_ref[0].astype(F32) + yb_ref[0].astype(F32)
    ssd_in = _rms(y * zs_ref[0].astype(F32), snw_ref[...]).astype(BF16)
    merged = merged + gate(0) * _dot(ssd_in, wso[...])

    def gated_in(ref):
        return ref[0, :, CONV_WIDTH:2 * CONV_WIDTH].astype(F32) * ref[0, :, 2 * CONV_WIDTH:].astype(F32)
    ext_u[0:HALO, :] = jnp.where(seq_start, 0.0, gated_in(cvp_ref))
    ext_u[HALO:HALO + tm, :] = gated_in(cv_ref)
    ext_u[HALO + tm:, :] = jnp.where(seq_end, 0.0, gated_in(cvn_ref))
    conv = cw_ref[0:1, :] * ext_u[HALO - 1:HALO - 1 + tm, :]
    for k in range(1, CONV_K):
        conv = conv + cw_ref[k:k + 1, :] * ext_u[HALO - 1 + k:HALO - 1 + k + tm, :]
    conv_in = (cv_ref[0, :, :CONV_WIDTH].astype(F32) * conv).astype(BF16)
    merged = merged + gate(3) * _dot(conv_in, wco[...])
    o_ref[0] = h_ref[0] + mod(5) * _dot(merged.astype(BF16), wo[...])


def _merge(h, ml, mc, yf, yb, zs, att_c, att_l, pv, cv, sg, w, stacked, layer, *, n_ctx):
    b, s, d = h.shape
    tm = MIX_TILE
    assert s % tm == 0 and n_ctx % tm == 0 and tm % HALO == 0
    ctx_tiles = n_ctx // tm
    tile = lambda i, t: (i, t, 0)
    prev, nxt = _halo_maps(tm, s)
    bands = _pool_bands(tm)
    st = lambda name: (stacked[name], layer)
    weights = [(w["ssd_norm"], None), st("ssd_w_out"), st("mla_w_out"), (jnp.asarray(bands, BF16), None),
               st("pool_w"), (w["pool_scale"], None), st("pool_w_out"), (w["sconv_w"], None),
               st("sconv_w_out"), st("w_o")]
    w_specs = [_resident(x.shape) if l is None else _layer_resident(x, l) for x, l in weights]
    w_bytes = sum(_nbytes(x.shape if l is None else x.shape[1:], x.dtype) for x, l in weights)
    stream = (2 * _nbytes((tm, d), F32) + _nbytes((tm, 4 * SSD_INNER + POOL_WIDTH + 3 * CONV_WIDTH + N_BRANCH * d), BF16)
              + 2 * _nbytes((HALO, POOL_WIDTH + 3 * CONV_WIDTH), BF16))
    est = 2 * stream + w_bytes + _nbytes((tm + 2 * HALO, d), F32) + 10 * _nbytes((tm, d), F32)
    return pl.pallas_call(
        functools.partial(_merge_kernel, n_ctx=n_ctx),
        grid=(b, s // tm),
        in_specs=[pl.BlockSpec((1, tm, d), tile),
                  pl.BlockSpec((1, N_MOD, d), lambda i, t: (i, 0, 0)),
                  _resident(mc.shape),
                  pl.BlockSpec((1, tm, SSD_INNER), tile),
                  pl.BlockSpec((1, tm, SSD_INNER), tile),
                  pl.BlockSpec((1, tm, SSD_INNER), tile),
                  pl.BlockSpec((1, tm, MLA_HEADS * MLA_V), lambda i, t: (i, jnp.minimum(t, ctx_tiles - 1), 0)),
                  pl.BlockSpec((1, tm, MLA_HEADS * MLA_V), lambda i, t: (i, jnp.maximum(t - ctx_tiles, 0), 0)),
                  pl.BlockSpec((1, tm, POOL_WIDTH), tile),
                  pl.BlockSpec((1, HALO, POOL_WIDTH), prev),
                  pl.BlockSpec((1, HALO, POOL_WIDTH), nxt),
                  pl.BlockSpec((1, tm, 3 * CONV_WIDTH), tile),
                  pl.BlockSpec((1, HALO, 3 * CONV_WIDTH), prev),
                  pl.BlockSpec((1, HALO, 3 * CONV_WIDTH), nxt),
                  pl.BlockSpec((1, tm, N_BRANCH * d), tile)]
                 + w_specs,
        out_specs=pl.BlockSpec((1, tm, d), tile),
        out_shape=jax.ShapeDtypeStruct(h.shape, F32),
        scratch_shapes=[pltpu.VMEM((tm + 2 * HALO, CONV_WIDTH), F32)],
        compiler_params=_params(est, 2),
        name="merge",
    )(h, ml, mc, yf, yb, zs, att_c, att_l, pv, pv, pv, cv, cv, cv, sg, *[x for x, _ in weights])


def _rope_tables(n_ctx, n_lat):
    rows = n_lat // GRID_W
    row = np.repeat(np.arange(rows), GRID_W).astype(np.float32)
    col = np.tile(np.arange(GRID_W), rows).astype(np.float32)
    half = MLA_ROPE // 2
    inv = (1.0 / (ROPE_THETA ** (jnp.arange(0, half, 2, dtype=F32) / half)))
    ar = jnp.asarray(row)[:, None] * inv
    ac = jnp.asarray(col)[:, None] * inv
    ang = jnp.concatenate([ar, ar, ac, ac], axis=-1)
    pad = LANES - MLA_ROPE
    cos = jnp.concatenate([jnp.ones((n_ctx, MLA_ROPE), F32), jnp.cos(ang)], axis=0)
    sin = jnp.concatenate([jnp.zeros((n_ctx, MLA_ROPE), F32), jnp.sin(ang)], axis=0)
    return jnp.pad(cos, ((0, 0), (0, pad))), jnp.pad(sin, ((0, 0), (0, pad)))


def _rot_cols(w):
    w1, w2, w3, w4 = jnp.split(w, 4, axis=-1)
    return jnp.concatenate([-w2, w1, -w4, w3], axis=-1)


def _layer_weights(i, p):
    w_in = p["w_in"][i]
    splits = np.cumsum([SSD_INNER, SSD_XBC, 2 * SSD_HEADS, MLA_Q_RANK, MLA_KV_RANK, MLA_ROPE,
                        POOL_WIDTH, 3 * CONV_WIDTH])
    wz, wxbc, wdt, wqd, wkvd, wkr, wpv, wcv, wg = jnp.split(w_in, splits, axis=1)
    lane_pad = lambda x: jnp.pad(x, ((0, 0), (0, LANES - x.shape[1])))
    uq = p["mla_w_uq"][i].reshape(MLA_Q_RANK, MLA_HEADS, MLA_QK)
    uq_pe = uq[:, :, MLA_NOPE:]
    head_pad = lambda x: jnp.pad(x, ((0, 0), (0, 0), (0, LANES - MLA_ROPE))).reshape(MLA_Q_RANK, MLA_HEADS * LANES)
    bf = lambda x: x.astype(BF16)
    return {
        "wzx": bf(w_in[:, :SSD_INNER + SSD_XBC]), "wloc": bf(w_in[:, splits[5]:]),
        "wsm": bf(jnp.concatenate([wqd, wkvd, lane_pad(wkr), lane_pad(_rot_cols(wkr)), lane_pad(wdt)], axis=1)),
        "q_norm": p["mla_q_norm"][i].reshape(1, -1), "kv_norm": p["mla_kv_norm"][i].reshape(1, -1),
        "wuqn": bf(uq[:, :, :MLA_NOPE].reshape(MLA_Q_RANK, MLA_HEADS * MLA_NOPE)),
        "wuqp": bf(head_pad(uq_pe)), "wuqr": bf(head_pad(_rot_cols(uq_pe))),
        "wuk": bf(p["mla_w_uk"][i]), "wuvt": bf(p["mla_w_uv"][i].T),
        "ssd_norm": p["ssd_norm"][i].reshape(1, -1), "pool_scale": p["pool_scale"][i].reshape(1, -1),
        "sconv_w": p["sconv_w"][i],
    }


def kernel(x, c, ctx, c_ctx, w_ada, b_ada, ffn1_norm, ffn1_w_gate, ffn1_w_up, ffn1_w_down, mix_norm, w_in, ssd_conv_w, ssd_conv_b, ssd_dt_bias, ssd_a_log, ssd_d, ssd_norm, ssd_w_out, mla_q_norm, mla_w_uq, mla_kv_norm, mla_w_uk, mla_w_uv, mla_w_out, pool_w, pool_scale, pool_w_out, sconv_w, sconv_w_out, w_o, ffn2_norm, ffn2_w_gate, ffn2_w_up, ffn2_w_down, final_norm):
    p = dict(w_in=w_in.astype(BF16), ssd_norm=ssd_norm, mla_q_norm=mla_q_norm, mla_w_uq=mla_w_uq,
             mla_kv_norm=mla_kv_norm, mla_w_uk=mla_w_uk, mla_w_uv=mla_w_uv, pool_scale=pool_scale, sconv_w=sconv_w)
    stacked = {name: a.astype(BF16) for name, a in dict(
        ssd_w_out=ssd_w_out, mla_w_out=mla_w_out, pool_w=pool_w, pool_w_out=pool_w_out,
        sconv_w_out=sconv_w_out, w_o=w_o).items()}
    bsz, n_lat, d = x.shape
    n_ctx = ctx.shape[1]
    depth = w_ada.shape[0]

    mod_rows = -(-(bsz + 1) // 8) * 8
    cc = jnp.concatenate([c, c_ctx[None, :], jnp.zeros((mod_rows - bsz - 1, d), F32)], axis=0)
    mods = _adaln(cc, w_ada, b_ada).reshape(depth, mod_rows, N_MOD, d)

    cos_t, sin_t = _rope_tables(n_ctx, n_lat)
    ffn1_w = [a.astype(BF16) for a in (ffn1_w_gate, ffn1_w_up, ffn1_w_down)]
    ffn2_w = [a.astype(BF16) for a in (ffn2_w_gate, ffn2_w_up, ffn2_w_down)]
    h = x
    for i in range(depth):
        ml, mc = mods[i, :bsz], mods[i, bsz]
        w = _layer_weights(i, p)
        h = _ffn(h, ml, mc, ffn1_norm[i], *ffn1_w, i, n_ctx=n_ctx, mod0=0, ctx=ctx if i == 0 else None)
        zs, xs, cm, bt, dt, qn, qp, kn, kp, vt = _proj_seq(
            h, ml, mc, mix_norm[i], cos_t, sin_t, ssd_conv_w[i], ssd_conv_b[i], w, n_ctx=n_ctx)
        pv, cv, sg = _proj_loc(h, ml, mc, mix_norm[i], w, n_ctx=n_ctx)
        yf, yb = _ssd_scan(xs, cm, bt, dt, ssd_dt_bias[i], ssd_a_log[i], ssd_d[i], n_ctx=n_ctx)
        att_c, att_l = _attention(qn, qp, kn, kp, vt, n_ctx=n_ctx)
        h = _merge(h, ml, mc, yf, yb, zs, att_c, att_l, pv, cv, sg, w, stacked, i, n_ctx=n_ctx)
        h = _ffn(h, ml, mc, ffn2_norm[i], *ffn2_w, i, n_ctx=n_ctx, mod0=6,
                 final_w=final_norm if i == depth - 1 else None)
    return h
```

```python
import functools
import math

import jax
import jax.numpy as jnp
import numpy as np
from jax import lax
from jax.experimental import pallas as pl
from jax.experimental.pallas import tpu as pltpu

F32 = jnp.float32
BF16 = jnp.bfloat16

EPS = 1e-6
LOG2E = math.log2(math.e)
FFN_RES = 0.5
N_MOD = 9
GRID_W = 64
ROPE_THETA = 10000.0

SSD_HEADS = 16
SSD_HEAD_DIM = 64
SSD_GROUPS = 4
SSD_STATE = 128
SSD_CONV = 4
SSD_CHUNK = 128
SSD_INNER = SSD_HEADS * SSD_HEAD_DIM
SSD_BC = SSD_GROUPS * SSD_STATE
SSD_XBC = SSD_INNER + 2 * SSD_BC
HEADS_PER_GROUP = SSD_HEADS // SSD_GROUPS

MLA_HEADS = 8
MLA_Q_RANK = 384
MLA_KV_RANK = 256
MLA_NOPE = 128
MLA_ROPE = 64
MLA_V = 128
MLA_QK = MLA_NOPE + MLA_ROPE

POOL_WINDOWS = (2, 4, 8, 16)
POOL_GROUP = 256
POOL_WIDTH = POOL_GROUP * len(POOL_WINDOWS)
CONV_WIDTH = 1024
CONV_K = 3
N_BRANCH = 4

LANES = 128
BF16_ROWS = 16
HALO = BF16_ROWS
VMEM_CAP = 56 * 1024 * 1024
VMEM_SLACK = 12 * 1024 * 1024

ROW_TILE = 768
MIX_TILE = 256
FFN_CHUNK = 256
ATTN_KEY_BLOCK = 256


def _params(est_bytes, n_axes):
    return pltpu.CompilerParams(
        dimension_semantics=("arbitrary",) * n_axes,
        vmem_limit_bytes=int(min(est_bytes + VMEM_SLACK, VMEM_CAP)))


def _resident(shape):
    nd = len(shape)
    return pl.BlockSpec(shape, lambda *_: (0,) * nd, pipeline_mode=pl.Buffered(1))


def _layer_resident(x, layer):
    nd = x.ndim - 1
    return pl.BlockSpec((None,) + x.shape[1:], lambda *_: (layer,) + (0,) * nd, pipeline_mode=pl.Buffered(1))


def _nbytes(shape, dtype):
    return int(np.prod(shape)) * jnp.dtype(dtype).itemsize


def _dot(a, b):
    return jnp.dot(a, b, preferred_element_type=F32)


def _dot_nt(a, b):
    return lax.dot_general(a, b, (((1,), (1,)), ((), ())), preferred_element_type=F32)


def _silu(x):
    return x * jax.nn.sigmoid(x)


def _rms(x, w):
    ms = jnp.mean(x * x, axis=-1, keepdims=True)
    return x * lax.rsqrt(ms + EPS) * w


def _row_mods(ml_ref, mc_ref, tile, tile_rows, n_ctx):
    rows = tile * tile_rows + lax.broadcasted_iota(jnp.int32, (tile_rows, 1), 0)
    is_ctx = rows < n_ctx

    def mod(k):
        return jnp.where(is_ctx, mc_ref[k:k + 1, :], ml_ref[0, k:k + 1, :])
    return mod


def _halo_maps(tile_rows, s_len):
    hb = tile_rows // HALO
    last = s_len // HALO - 1
    prev = lambda i, t: (i, jnp.maximum(t * hb - 1, 0), 0)
    nxt = lambda i, t: (i, jnp.minimum((t + 1) * hb, last), 0)
    return prev, nxt


def _normed_rows(ref, first_row, ml_ref, mc_ref, nw_ref, n_ctx):
    rows = first_row + lax.broadcasted_iota(jnp.int32, (ref.shape[1], 1), 0)
    is_ctx = rows < n_ctx
    mod = lambda k: jnp.where(is_ctx, mc_ref[k:k + 1, :], ml_ref[0, k:k + 1, :])
    return (_rms(ref[0], nw_ref[...]) * (1.0 + mod(4)) + mod(3)).astype(BF16)


def _tap_masker(row0, tm, n_ctx, s_len):
    sub = 8
    starts = sorted({0, n_ctx % tm // sub * sub})
    ends = sorted({(n_ctx - 1) % tm // sub * sub, (tm - 1) // sub * sub})

    def masked(tap, o):
        pieces, r = [], 0
        for g in (starts if o < 0 else ends):
            rows = row0 + g + lax.broadcasted_iota(jnp.int32, (sub, LANES), 0)
            in_ctx = rows < n_ctx
            pos = jnp.where(in_ctx, rows, rows - n_ctx)
            valid = pos + o >= 0 if o < 0 else pos + o < jnp.where(in_ctx, n_ctx, s_len - n_ctx)
            pieces += [tap[r:g], jnp.where(valid, tap[g:g + sub], 0.0)]
            r = g + sub
        return jnp.concatenate([p for p in pieces + [tap[r:]] if p.shape[0]], axis=0)
    return masked


def _seq_edges(tile, tile_rows, n_ctx, s_len):
    row0 = tile * tile_rows
    start = jnp.logical_or(row0 == 0, row0 == n_ctx)
    end = jnp.logical_or(row0 + tile_rows == n_ctx, row0 + tile_rows == s_len)
    return start, end


def _ada_kernel(c_ref, w_ref, b_ref, o_ref):
    s = _silu(c_ref[...]).astype(BF16)
    o_ref[0] = _dot(s, w_ref[0].astype(BF16)) + b_ref[0]


def _adaln(cc, w_ada, b_ada):
    depth, d, n = w_ada.shape
    rows = cc.shape[0]
    tn = n // 8
    est = 2 * (_nbytes((d, tn), F32) + _nbytes((rows, tn), F32)) + _nbytes((d, tn), BF16)
    return pl.pallas_call(
        _ada_kernel,
        grid=(depth, n // tn),
        in_specs=[pl.BlockSpec((rows, d), lambda i, j: (0, 0)),
                  pl.BlockSpec((1, d, tn), lambda i, j: (i, 0, j)),
                  pl.BlockSpec((1, 1, tn), lambda i, j: (i, 0, j))],
        out_specs=pl.BlockSpec((1, rows, tn), lambda i, j: (i, 0, j)),
        out_shape=jax.ShapeDtypeStruct((depth, rows, n), F32),
        compiler_params=_params(est, 2),
        name="adaln",
    )(cc, w_ada, b_ada.reshape(depth, 1, n))


def _ffn_body(h, mod, mod0, nw_ref, wg_ref, wu_ref, wd_ref, p_scr):
    d_ff = wg_ref.shape[1]
    a = (_rms(h, nw_ref[...]) * (1.0 + mod(mod0 + 1)) + mod(mod0)).astype(BF16)
    for j in range(d_ff // FFN_CHUNK):
        cols = slice(j * FFN_CHUNK, (j + 1) * FFN_CHUNK)
        g = _dot(a, wg_ref[:, cols])
        u = _dot(a, wu_ref[:, cols])
        p_scr[:, cols] = (_silu(g) * u).astype(BF16)
    y = _dot(p_scr[...], wd_ref[...])
    return h + (FFN_RES * mod(mod0 + 2)) * y


def _ffn_kernel(h_ref, ml_ref, mc_ref, nw_ref, wg_ref, wu_ref, wd_ref, o_ref, p_scr, *, n_ctx, mod0):
    tm = h_ref.shape[1]
    mod = _row_mods(ml_ref, mc_ref, pl.program_id(1), tm, n_ctx)
    o_ref[0] = _ffn_body(h_ref[0], mod, mod0, nw_ref, wg_ref, wu_ref, wd_ref, p_scr)


def _ffn_first_kernel(ctx_ref, x_ref, ml_ref, mc_ref, nw_ref, wg_ref, wu_ref, wd_ref, o_ref, p_scr, *, n_ctx, mod0):
    tm = x_ref.shape[1]
    t = pl.program_id(1)
    mod = _row_mods(ml_ref, mc_ref, t, tm, n_ctx)
    h = jnp.where(t < n_ctx // tm, ctx_ref[0], x_ref[0])
    o_ref[0] = _ffn_body(h, mod, mod0, nw_ref, wg_ref, wu_ref, wd_ref, p_scr)


def _ffn_last_kernel(h_ref, ml_ref, mc_ref, nw_ref, wg_ref, wu_ref, wd_ref, fw_ref, o_ref, p_scr, *, n_ctx, mod0):
    tm = h_ref.shape[1]
    mod = _row_mods(ml_ref, mc_ref, pl.program_id(1), tm, n_ctx)
    o_ref[0] = _rms(_ffn_body(h_ref[0], mod, mod0, nw_ref, wg_ref, wu_ref, wd_ref, p_scr), fw_ref[...])


def _ffn(h, ml, mc, norm_w, wg, wu, wd, layer, *, n_ctx, mod0, ctx=None, final_w=None):
    b, _, d = h.shape
    d_ff = wg.shape[2]
    layer_block = lambda x: _layer_resident(x, layer)
    special = ctx is not None or final_w is not None
    tm = MIX_TILE if special else ROW_TILE
    s = h.shape[1] + (ctx.shape[1] if ctx is not None else 0)
    assert s % tm == 0 and d_ff % FFN_CHUNK == 0 and (not special or n_ctx % tm == 0)
    ctx_tiles = n_ctx // tm
    est = (4 * _nbytes((tm, d), F32) + 3 * _nbytes((d, d_ff), BF16) + _nbytes((tm, d_ff), BF16)
           + 4 * _nbytes((tm, FFN_CHUNK), F32) + 2 * _nbytes((tm, d), F32))
    tile = lambda i, t: (i, t, 0)
    lat_tile = lambda i, t: (i, jnp.maximum(t - ctx_tiles, 0), 0)
    common = [pl.BlockSpec((1, N_MOD, d), lambda i, t: (i, 0, 0)),
              _resident(mc.shape), _resident((1, d)),
              layer_block(wg), layer_block(wu), layer_block(wd)]
    args = [ml, mc, norm_w.reshape(1, d), wg, wu, wd]
    out_rows, out_spec = s, pl.BlockSpec((1, tm, d), tile)
    if ctx is not None:
        body = _ffn_first_kernel
        in_specs = [pl.BlockSpec((1, tm, d), lambda i, t: (i, jnp.minimum(t, ctx_tiles - 1), 0)),
                    pl.BlockSpec((1, tm, d), lat_tile)] + common
        args = [ctx, h] + args
    elif final_w is not None:
        body = _ffn_last_kernel
        in_specs = [pl.BlockSpec((1, tm, d), tile)] + common + [_resident((1, d))]
        args = [h] + args + [final_w.reshape(1, d)]
        out_rows, out_spec = s - n_ctx, pl.BlockSpec((1, tm, d), lat_tile)
    else:
        body = _ffn_kernel
        in_specs = [pl.BlockSpec((1, tm, d), tile)] + common
        args = [h] + args
    return pl.pallas_call(
        functools.partial(body, n_ctx=n_ctx, mod0=mod0),
        grid=(b, s // tm),
        in_specs=in_specs,
        out_specs=out_spec,
        out_shape=jax.ShapeDtypeStruct((b, out_rows, d), F32),
        scratch_shapes=[pltpu.VMEM((tm, d_ff), BF16)],
        compiler_params=_params(est, 2),
        name="ffn",
    )(*args)


def _proj_seq_kernel(h_ref, hp_ref, hn_ref, ml_ref, mc_ref, nw_ref, cos_ref, sin_ref, cw_ref, cb_ref,
                     wzx, wsm, qnw_ref, kvnw_ref, wuqn, wuqp, wuqr, wuk, wuvt,
                     zs_o, xs_o, c_o, bt_o, dt_o, qn_o, qp_o, kn_o, kp_o, vt_o, *, n_ctx, qk_scale):
    tm = h_ref.shape[1]
    t = pl.program_id(1)
    s_len = pl.num_programs(1) * tm
    row0 = t * tm

    def normed(ref, first_row):
        rows = first_row + lax.broadcasted_iota(jnp.int32, (ref.shape[1], 1), 0)
        is_ctx = rows < n_ctx
        mod = lambda k: jnp.where(is_ctx, mc_ref[k:k + 1, :], ml_ref[0, k:k + 1, :])
        return (_rms(ref[0], nw_ref[...]) * (1.0 + mod(4)) + mod(3)).astype(BF16)

    a = normed(h_ref, row0)
    a_ext = jnp.concatenate([normed(hp_ref, row0 - HALO), a, normed(hn_ref, row0 + tm)], axis=0)
    xbc = _dot(a_ext, wzx[:, SSD_INNER:])
    zs_o[0] = _silu(_dot(a, wzx[:, :SSD_INNER])).astype(BF16)
    sm = _dot(a, wsm[...])
    c0 = MLA_Q_RANK + MLA_KV_RANK
    qd, kvd, kr, kr_rot = sm[:, :MLA_Q_RANK], sm[:, MLA_Q_RANK:c0], sm[:, c0:c0 + LANES], sm[:, c0 + LANES:c0 + 2 * LANES]
    dt_o[0] = sm[:, c0 + 2 * LANES:]
    cos = cos_ref[...]
    sin = sin_ref[...]
    cos_h = jnp.concatenate([cos] * MLA_HEADS, axis=1)
    sin_h = jnp.concatenate([sin] * MLA_HEADS, axis=1)
    qn = _rms(qd, qnw_ref[...]).astype(BF16)
    qn_o[0] = (_dot(qn, wuqn[...]) * qk_scale).astype(BF16)
    qp_o[0] = ((_dot(qn, wuqp[...]) * cos_h + _dot(qn, wuqr[...]) * sin_h) * qk_scale).astype(BF16)
    ckv = _rms(kvd, kvnw_ref[...]).astype(BF16)
    kn_o[0] = _dot(ckv, wuk[...]).astype(BF16)
    vt_o[0] = _dot_nt(wuvt[...], ckv).astype(BF16)
    kp_o[0] = (kr * cos + kr_rot * sin).astype(BF16)

    ext_rows = tm + 2 * HALO
    pad_l = SSD_CONV // 2
    rows = row0 + lax.broadcasted_iota(jnp.int32, (tm, LANES), 0)
    in_ctx = rows < n_ctx
    pos = jnp.where(in_ctx, rows, rows - n_ctx)
    seq_n = jnp.where(in_ctx, n_ctx, s_len - n_ctx)
    valid = [pos + (k - pad_l) >= 0 if k < pad_l else pos + (k - pad_l) < seq_n for k in range(SSD_CONV)]
    sub = 8
    starts = sorted({0, n_ctx % tm // sub * sub})
    ends = sorted({(n_ctx - 1) % tm // sub * sub, (tm - 1) // sub * sub})

    def masked(tap, k):
        pieces, r = [], 0
        for g in (starts if k < pad_l else ends):
            pieces += [tap[r:g], jnp.where(valid[k][g:g + sub], tap[g:g + sub], 0.0)]
            r = g + sub
        return jnp.concatenate([p for p in pieces + [tap[r:]] if p.shape[0]], axis=0)

    q, n = SSD_CHUNK, SSD_STATE
    for j in range(SSD_XBC // LANES):
        lc = slice(j * LANES, (j + 1) * LANES)
        ext = xbc[:, lc]
        acc = cb_ref[:, lc] + cw_ref[pad_l:pad_l + 1, lc] * ext[HALO:HALO + tm]
        for k in range(SSD_CONV):
            if k != pad_l:
                tap = pltpu.roll(ext, (pad_l - k) % ext_rows, axis=0)[HALO:HALO + tm]
                acc = acc + cw_ref[k:k + 1, lc] * masked(tap, k)
        act = _silu(acc)
        col = j * LANES
        if col < SSD_INNER:
            xs_o[0, :, lc] = act.astype(BF16)
        elif col < SSD_INNER + SSD_BC:
            g = (col - SSD_INNER) // n
            for ci in range(tm // q):
                r0 = (ci * SSD_GROUPS + g) * n
                bt_o[0, r0:r0 + n, :] = act[ci * q:(ci + 1) * q, :].T.astype(BF16)
        else:
            cc = col - SSD_INNER - SSD_BC
            c_o[0, :, cc:cc + LANES] = act.astype(BF16)


def _proj_seq(h, ml, mc, norm_w, cos_t, sin_t, conv_w, conv_b, w, *, n_ctx):
    b, s, d = h.shape
    tm = ROW_TILE
    q = SSD_CHUNK
    assert s % tm == 0 and tm % q == 0 and tm % HALO == 0 and n_ctx % HALO == 0 and SSD_STATE == LANES
    assert MLA_Q_RANK % LANES == 0 and MLA_KV_RANK % LANES == 0
    weights = [w["wzx"], w["wsm"], w["q_norm"], w["kv_norm"],
               w["wuqn"], w["wuqp"], w["wuqr"], w["wuk"], w["wuvt"]]
    widths = [(SSD_INNER, BF16), (SSD_INNER, BF16), (SSD_BC, BF16), None, (LANES, F32),
              (MLA_HEADS * MLA_NOPE, BF16), (MLA_HEADS * LANES, BF16), (MLA_HEADS * MLA_NOPE, BF16), (LANES, BF16)]
    tile = lambda i, t: (i, t, 0)
    prev, nxt = _halo_maps(tm, s)
    vw = MLA_HEADS * MLA_V
    bt_rows = tm // q * SSD_BC
    row_outs = [x for x in widths if x is not None]
    est = (2 * _nbytes((tm + 2 * HALO, d), F32) + sum(_nbytes(x.shape, x.dtype) for x in weights)
           + 2 * sum(_nbytes((tm, n), dt) for n, dt in row_outs) + 2 * _nbytes((vw, tm), BF16)
           + 2 * _nbytes((bt_rows, q), BF16) + 4 * _nbytes((tm + 2 * HALO, SSD_XBC), F32))
    out_specs, out_shape = [], []
    for x in widths:
        if x is None:
            out_specs.append(pl.BlockSpec((1, bt_rows, q), tile))
            out_shape.append(jax.ShapeDtypeStruct((b, s // q * SSD_BC, q), BF16))
        else:
            out_specs.append(pl.BlockSpec((1, tm, x[0]), tile))
            out_shape.append(jax.ShapeDtypeStruct((b, s, x[0]), x[1]))
    return pl.pallas_call(
        functools.partial(_proj_seq_kernel, n_ctx=n_ctx, qk_scale=MLA_QK ** -0.5 * LOG2E),
        grid=(b, s // tm),
        in_specs=[pl.BlockSpec((1, tm, d), tile),
                  pl.BlockSpec((1, HALO, d), prev),
                  pl.BlockSpec((1, HALO, d), nxt),
                  pl.BlockSpec((1, N_MOD, d), lambda i, t: (i, 0, 0)),
                  _resident(mc.shape), _resident((1, d)),
                  pl.BlockSpec((tm, LANES), lambda i, t: (t, 0)),
                  pl.BlockSpec((tm, LANES), lambda i, t: (t, 0)),
                  _resident(conv_w.shape), _resident((1, SSD_XBC))]
                 + [_resident(x.shape) for x in weights],
        out_specs=out_specs + [pl.BlockSpec((1, vw, tm), lambda i, t: (i, 0, t))],
        out_shape=out_shape + [jax.ShapeDtypeStruct((b, vw, s), BF16)],
        compiler_params=_params(est, 2),
        name="proj_seq",
    )(h, h, h, ml, mc, norm_w.reshape(1, d), cos_t, sin_t, conv_w, conv_b.reshape(1, -1), *weights)


def _proj_loc_kernel(h_ref, hp_ref, hn_ref, ml_ref, mc_ref, nw_ref, scw_ref, wloc, pv_o, ci_o, g_o, *, n_ctx):
    tm = h_ref.shape[1]
    t = pl.program_id(1)
    s_len = pl.num_programs(1) * tm
    row0 = t * tm
    normed = functools.partial(_normed_rows, ml_ref=ml_ref, mc_ref=mc_ref, nw_ref=nw_ref, n_ctx=n_ctx)
    a = normed(h_ref, row0)
    a_ext = jnp.concatenate([normed(hp_ref, row0 - HALO), a, normed(hn_ref, row0 + tm)], axis=0)
    c1 = POOL_WIDTH + 3 * CONV_WIDTH
    cv = _dot(a_ext, wloc[:, POOL_WIDTH:c1])
    g_o[0] = jax.nn.sigmoid(_dot(a, wloc[:, c1:])).astype(BF16)
    pv_o[0] = _dot(a, wloc[:, :POOL_WIDTH]).astype(BF16)

    ext_rows = tm + 2 * HALO
    pad_l = CONV_K // 2
    masked = _tap_masker(row0, tm, n_ctx, s_len)
    for j in range(CONV_WIDTH // LANES):
        lc = slice(j * LANES, (j + 1) * LANES)
        u = (cv[:, CONV_WIDTH + j * LANES:CONV_WIDTH + (j + 1) * LANES]
             * cv[:, 2 * CONV_WIDTH + j * LANES:2 * CONV_WIDTH + (j + 1) * LANES])
        conv = scw_ref[pad_l:pad_l + 1, lc] * u[HALO:HALO + tm]
        for k in range(CONV_K):
            if k != pad_l:
                tap = pltpu.roll(u, (pad_l - k) % ext_rows, axis=0)[HALO:HALO + tm]
                conv = conv + scw_ref[k:k + 1, lc] * masked(tap, k - pad_l)
        ci_o[0, :, lc] = (cv[HALO:HALO + tm, lc] * conv).astype(BF16)


def _proj_loc(h, ml, mc, norm_w, sconv_w, w, *, n_ctx):
    b, s, d = h.shape
    tm = ROW_TILE
    assert s % tm == 0 and tm % HALO == 0 and n_ctx % HALO == 0
    wloc = w["wloc"]
    widths = [POOL_WIDTH, CONV_WIDTH, N_BRANCH * d]
    assert POOL_WIDTH + 3 * CONV_WIDTH + N_BRANCH * d == wloc.shape[1]
    tile = lambda i, t: (i, t, 0)
    prev, nxt = _halo_maps(tm, s)
    est = (2 * _nbytes((tm + 2 * HALO, d), F32) + _nbytes(wloc.shape, BF16)
           + 2 * sum(_nbytes((tm, n), BF16) for n in widths) + 2 * _nbytes((tm, N_BRANCH * d), F32)
           + 2 * _nbytes((tm + 2 * HALO, 3 * CONV_WIDTH), F32))
    return pl.pallas_call(
        functools.partial(_proj_loc_kernel, n_ctx=n_ctx),
        grid=(b, s // tm),
        in_specs=[pl.BlockSpec((1, tm, d), tile),
                  pl.BlockSpec((1, HALO, d), prev),
                  pl.BlockSpec((1, HALO, d), nxt),
                  pl.BlockSpec((1, N_MOD, d), lambda i, t: (i, 0, 0)),
                  _resident(mc.shape), _resident((1, d)), _resident(sconv_w.shape), _resident(wloc.shape)],
        out_specs=[pl.BlockSpec((1, tm, n), tile) for n in widths],
        out_shape=[jax.ShapeDtypeStruct((b, s, n), BF16) for n in widths],
        compiler_params=_params(est, 2),
        name="proj_loc",
    )(h, h, h, ml, mc, norm_w.reshape(1, d), sconv_w, wloc)


def _split_bf16(x, parts):
    out = []
    for _ in range(parts):
        p = x.astype(BF16)
        out.append(p)
        x = x - p.astype(F32)
    return out


class _SsdChunk:
    def __init__(self, d, sub, xs_ref, c_ref, bt_ref, dt_ref, y_ref, st_ref, bias_ref, alog_ref, dskip_ref,
                 expand_ref, tri_ref):
        self.d = d
        self.rows = slice(sub * SSD_CHUNK, (sub + 1) * SSD_CHUNK)
        self.bt_row0 = sub * SSD_BC
        self.xs_ref, self.c_ref, self.bt_ref, self.dt_ref = xs_ref, c_ref, bt_ref, dt_ref
        self.y_ref, self.st_ref = y_ref, st_ref
        self.bias_ref, self.alog_ref, self.dskip_ref = bias_ref, alog_ref, dskip_ref
        self.expand_ref, self.tri_ref = expand_ref, tri_ref

    def cumsum(self):
        x = self.dt_ref[0, self.rows, :] + self.bias_ref[...]
        self.dtv = jnp.maximum(x, 0.0) + jnp.log1p(jnp.exp(-jnp.abs(x)))
        da = self.dtv * (-LOG2E * jnp.exp(self.alog_ref[...]))
        tri = self.tri_ref[self.d]
        self.cs = sum(_dot(tri, p) for p in _split_bf16(da, 3))

    def expand(self):
        q = SSD_CHUNK
        last = q - 1 if self.d == 0 else 0
        cs, dtv = self.cs, self.dtv
        tot = cs[last:last + 1, :]
        self.src_t = (cs - jnp.log2(dtv)).T
        stack = jnp.concatenate(
            [dtv * jnp.exp2(tot - cs), jnp.exp2(cs), jnp.broadcast_to(jnp.exp2(tot), (BF16_ROWS, LANES))], axis=0)
        ex = _dot(stack.astype(BF16), self.expand_ref[self.d])
        self.ecs_exp = ex[q:2 * q]
        self.etot_exp = ex[2 * q:2 * q + 1]
        self.xs = self.xs_ref[0, self.rows, :]
        self.xe = (self.xs.astype(F32) * ex[:q]).astype(BF16)

    def group(self, g):
        d, q, n = self.d, SSD_CHUNK, SSD_STATE
        gw = HEADS_PER_GROUP * SSD_HEAD_DIM
        li = lax.broadcasted_iota(jnp.int32, (q, q), 0)
        si = lax.broadcasted_iota(jnp.int32, (q, q), 1)
        causal = (li >= si) if d == 0 else (li <= si)
        head_of_lane = lax.broadcasted_iota(jnp.int32, (q, gw), 1) // SSD_HEAD_DIM
        gc = slice(g * gw, (g + 1) * gw)
        c_g = self.c_ref[0, self.rows, g * n:(g + 1) * n]
        bt_g = self.bt_ref[0, self.bt_row0 + g * n:self.bt_row0 + (g + 1) * n, :]
        cb = _dot(c_g, bt_g)
        st_g = self.st_ref[:, gc]
        y_off = _dot(c_g, st_g.astype(BF16))
        xs_g = self.xs[:, gc]
        m_parts, x_parts = [], []
        for r in range(HEADS_PER_GROUP):
            col = d * SSD_HEADS + g * HEADS_PER_GROUP + r
            seg = self.cs[:, col:col + 1] - self.src_t[col:col + 1, :]
            m_parts.append((cb * jnp.exp2(jnp.where(causal, seg, -jnp.inf))).astype(BF16))
            x_parts.append(jnp.where(head_of_lane == r, xs_g, jnp.zeros_like(xs_g)))
        y_g = _dot(jnp.concatenate(m_parts, axis=1), jnp.concatenate(x_parts, axis=0))
        y_g = y_g + y_off * self.ecs_exp[:, gc]
        if d == 0:
            y_g = y_g + xs_g.astype(F32) * self.dskip_ref[:, gc]
        self.y_ref[0, self.rows, gc] = y_g.astype(BF16)
        self.st_ref[:, gc] = st_g * self.etot_exp[:, gc] + _dot(bt_g, self.xe[:, gc])


SCAN_CHUNKS = 2


def _ssd_scan_kernel(xs_f, c_f, bt_f, dt_f, xs_b, c_b, bt_b, dt_b, bias_ref, alog_ref, dskip_ref, expand_ref,
                     tri_ref, yf_ref, yb_ref, stf_s, stb_s):
    @pl.when(pl.program_id(1) == 0)
    def _reset():
        stf_s[...] = jnp.zeros(stf_s.shape, F32)
        stb_s[...] = jnp.zeros(stb_s.shape, F32)
    shared = (bias_ref, alog_ref, dskip_ref, expand_ref, tri_ref)
    waves = [(_SsdChunk(0, k, xs_f, c_f, bt_f, dt_f, yf_ref, stf_s, *shared),
              _SsdChunk(1, SCAN_CHUNKS - 1 - k, xs_b, c_b, bt_b, dt_b, yb_ref, stb_s, *shared))
             for k in range(SCAN_CHUNKS)]
    for wave in waves:
        for ch in wave:
            ch.cumsum()
    for wave in waves:
        for ch in wave:
            ch.expand()
    for wave in waves:
        for g in range(SSD_GROUPS):
            for ch in wave:
                ch.group(g)


def _ssd_scan(xs, cm, bt, dt, dt_bias, a_log, d_skip, *, n_ctx):
    b, s, _ = xs.shape
    q = SSD_CHUNK
    rows = SCAN_CHUNKS * q
    assert s % rows == 0 and n_ctx % rows == 0
    nb = s // rows
    nbt = n_ctx // rows
    pad = LANES - 2 * SSD_HEADS
    bias_row = jnp.pad(dt_bias.reshape(1, -1), ((0, 0), (0, pad)))
    alog_row = jnp.pad(a_log.reshape(1, -1), ((0, 0), (0, pad)))
    dskip_row = jnp.repeat(d_skip, SSD_HEAD_DIM).reshape(1, SSD_INNER)
    e = np.zeros((2, LANES, SSD_INNER), np.float32)
    for d in range(2):
        for h in range(SSD_HEADS):
            e[d, d * SSD_HEADS + h, h * SSD_HEAD_DIM:(h + 1) * SSD_HEAD_DIM] = 1.0
    low = np.tril(np.ones((q, q), np.float32))
    tri = np.stack([low, low.T])
    fwd = lambda i, c: (i, c, 0)
    bwd = lambda i, c: (i, jnp.where(c < nbt, nbt - 1 - c, nb - 1 - c + nbt), 0)
    per_dir = lambda m: [pl.BlockSpec((1, rows, SSD_INNER), m), pl.BlockSpec((1, rows, SSD_BC), m),
                         pl.BlockSpec((1, SCAN_CHUNKS * SSD_BC, q), m), pl.BlockSpec((1, rows, LANES), m)]
    est = (4 * (_nbytes((rows, SSD_INNER + SSD_BC), BF16) + _nbytes((SCAN_CHUNKS * SSD_BC, q), BF16)
                + _nbytes((rows, LANES), F32))
           + 4 * _nbytes((rows, SSD_INNER), BF16) + 2 * _nbytes((SSD_STATE, SSD_INNER), F32)
           + _nbytes(e.shape, BF16) + 12 * SCAN_CHUNKS * _nbytes((q, SSD_INNER), F32))
    return pl.pallas_call(
        _ssd_scan_kernel,
        grid=(b, nb),
        in_specs=per_dir(fwd) + per_dir(bwd)
                 + [_resident((1, LANES)), _resident((1, LANES)), _resident((1, SSD_INNER)),
                    _resident(e.shape), _resident(tri.shape)],
        out_specs=[pl.BlockSpec((1, rows, SSD_INNER), fwd), pl.BlockSpec((1, rows, SSD_INNER), bwd)],
        out_shape=[jax.ShapeDtypeStruct((b, s, SSD_INNER), BF16)] * 2,
        scratch_shapes=[pltpu.VMEM((SSD_STATE, SSD_INNER), F32)] * 2,
        compiler_params=_params(est, 2),
        name="ssd_scan",
    )(xs, cm, bt, dt, xs, cm, bt, dt, bias_row, alog_row, dskip_row, jnp.asarray(e, BF16), jnp.asarray(tri, BF16))


def _attn_kernel(*refs, q_blocks):
    qn_refs, qp_refs = refs[:q_blocks], refs[q_blocks:2 * q_blocks]
    kn_ref, kp_ref, vt_ref, o_ref, st_scr, p_scr = refs[2 * q_blocks:]
    kp = kp_ref[0]
    n_keys, n_q = p_scr.shape
    kb = ATTN_KEY_BLOCK
    sub = 8

    def scores(h):
        hc = slice(h * LANES, (h + 1) * LANES)
        qc = jnp.concatenate(
            [jnp.concatenate([qn[0, :, hc], qp[0, :, hc]], axis=1) for qn, qp in zip(qn_refs, qp_refs)], axis=0)
        kc = jnp.concatenate([kn_ref[0, :, hc], kp], axis=1)
        st_scr[h % 2] = _dot_nt(kc, qc)

    scores(0)
    for h in range(MLA_HEADS):
        if h + 1 < MLA_HEADS:
            scores(h + 1)
        st = st_scr.at[h % 2]
        hc = slice(h * LANES, (h + 1) * LANES)
        blocks = [slice(c * kb, (c + 1) * kb) for c in range(n_keys // kb)]
        m8 = None
        for rows in blocks:
            bm = jnp.max(st[rows, :].reshape(kb // sub, sub, n_q), axis=0)
            m8 = bm if m8 is None else jnp.maximum(m8, bm)
        m = jnp.max(m8, axis=0, keepdims=True)
        l8 = jnp.zeros((sub, n_q), F32)
        for rows in blocks:
            p = jnp.exp2(st[rows, :] - m)
            l8 = l8 + jnp.sum(p.reshape(kb // sub, sub, n_q), axis=0)
            p_scr[rows, :] = p.astype(BF16)
        denom = jnp.sum(l8, axis=0, keepdims=True)
        ot = _dot(vt_ref[0, hc, :], p_scr[...])
        o_ref[0, :, hc] = (ot / denom).T.astype(BF16)


def _attention(qn, qp, kn, kp, vt, *, n_ctx):
    b, s, w = qn.shape
    tb = MIX_TILE
    lat_q_blocks = 2
    tq = tb * lat_q_blocks
    assert n_ctx == tb and (s - n_ctx) % tq == 0 and MLA_V == LANES
    ctx_blocks = n_ctx // tb

    def call(q_blocks, n_keys, q_maps, n_tiles, name):
        rows = q_blocks * tb
        whole = lambda i, t: (i, 0, 0)
        est = (4 * _nbytes((rows, w), BF16) + 2 * (2 * _nbytes((n_keys, w), BF16) + _nbytes((n_keys, LANES), BF16))
               + 2 * _nbytes((rows, w), BF16) + 3 * _nbytes((n_keys, rows), F32) + _nbytes((n_keys, 2 * LANES), BF16))
        return pl.pallas_call(
            functools.partial(_attn_kernel, q_blocks=q_blocks),
            grid=(b, n_tiles),
            in_specs=[pl.BlockSpec((1, tb, w), m) for m in q_maps] * 2
                     + [pl.BlockSpec((1, n_keys, w), whole), pl.BlockSpec((1, n_keys, LANES), whole),
                        pl.BlockSpec((1, w, n_keys), whole)],
            out_specs=pl.BlockSpec((1, rows, w), lambda i, t: (i, t, 0)),
            out_shape=jax.ShapeDtypeStruct((b, n_tiles * rows, w), BF16),
            scratch_shapes=[pltpu.VMEM((2, n_keys, rows), F32), pltpu.VMEM((n_keys, rows), BF16)],
            compiler_params=_params(est, 2),
            name=name,
        )(*([qn] * q_blocks), *([qp] * q_blocks), kn, kp, vt)

    att_c = call(1, n_ctx, [lambda i, t: (i, 0, 0)], 1, "attention_ctx")
    lat_maps = [lambda i, t, j=j: (i, ctx_blocks + lat_q_blocks * t + j, 0) for j in range(lat_q_blocks)]
    att_l = call(lat_q_blocks, s, lat_maps, (s - n_ctx) // tq, "attention")
    return att_c, att_l


def _pool_bands(tile_rows):
    t = np.arange(tile_rows)[:, None]
    j = np.arange(tile_rows + 2 * HALO)[None, :]
    bands = []
    for win in POOL_WINDOWS:
        lo = HALO + t - win // 2
        bands.append(((j >= lo) & (j < lo + win)).astype(np.float32))
    return np.stack(bands)


def _merge_kernel(h_ref, ml_ref, mc_ref, yf_ref, yb_ref, zs_ref, attc_ref, attl_ref,
                  pv_ref, pvp_ref, pvn_ref, ci_ref, sg_ref,
                  snw_ref, wso, wmo, band_ref, pw_ref, ps_ref, wpo, wco, wo,
                  o_ref, *, n_ctx):
    tm = h_ref.shape[1]
    d = h_ref.shape[2]
    t = pl.program_id(1)
    s_len = pl.num_programs(1) * tm
    mod = _row_mods(ml_ref, mc_ref, t, tm, n_ctx)
    seq_start, seq_end = _seq_edges(t, tm, n_ctx, s_len)

    def gate(k):
        return sg_ref[0, :, k * d:(k + 1) * d].astype(F32)

    mla_b = _dot(jnp.where(t < n_ctx // tm, attc_ref[0], attl_ref[0]), wmo[...])
    merged = gate(1) * mla_b

    pv = pv_ref[0]
    ext_p = jnp.concatenate([jnp.where(seq_start, jnp.zeros_like(pvp_ref[0]), pvp_ref[0]), pv,
                             jnp.where(seq_end, jnp.zeros_like(pvn_ref[0]), pvn_ref[0])], axis=0)
    rows = t * tm + lax.broadcasted_iota(jnp.int32, (tm, 1), 0)
    in_ctx = rows < n_ctx
    pos = jnp.where(in_ctx, rows, rows - n_ctx)
    seq_n = jnp.where(in_ctx, n_ctx, s_len - n_ctx)
    pooled = []
    for gi, win in enumerate(POOL_WINDOWS):
        gc = slice(gi * POOL_GROUP, (gi + 1) * POOL_GROUP)
        left = win // 2
        lo = jnp.clip(pos - left, 0, seq_n)
        hi = jnp.clip(pos - left + win, 0, seq_n)
        p_g = _dot(band_ref[gi], ext_p[:, gc]) / (hi - lo).astype(F32) - pv[:, gc].astype(F32)
        pooled.append(_dot(p_g.astype(BF16), pw_ref[gi]))
    pool_y = jnp.concatenate(pooled, axis=1) * ps_ref[...]
    merged = merged + gate(2) * _dot(pool_y.astype(BF16), wpo[...])

    y = yf_ref[0].astype(F32) + yb_ref[0].astype(F32)
    ssd_in = _rms(y * zs_ref[0].astype(F32), snw_ref[...]).astype(BF16)
    merged = merged + gate(0) * _dot(ssd_in, wso[...])

    merged = merged + gate(3) * _dot(ci_ref[0], wco[...])
    o_ref[0] = h_ref[0] + mod(5) * _dot(merged.astype(BF16), wo[...])


def _merge(h, ml, mc, yf, yb, zs, att_c, att_l, pv, cv, sg, w, stacked, layer, *, n_ctx):
    b, s, d = h.shape
    tm = MIX_TILE
    assert s % tm == 0 and n_ctx % tm == 0 and tm % HALO == 0
    ctx_tiles = n_ctx // tm
    tile = lambda i, t: (i, t, 0)
    prev, nxt = _halo_maps(tm, s)
    bands = _pool_bands(tm)
    st = lambda name: (stacked[name], layer)
    weights = [(w["ssd_norm"], None), st("ssd_w_out"), st("mla_w_out"), (jnp.asarray(bands, BF16), None),
               st("pool_w"), (w["pool_scale"], None), st("pool_w_out"), st("sconv_w_out"), st("w_o")]
    w_specs = [_resident(x.shape) if l is None else _layer_resident(x, l) for x, l in weights]
    w_bytes = sum(_nbytes(x.shape if l is None else x.shape[1:], x.dtype) for x, l in weights)
    stream = (2 * _nbytes((tm, d), F32) + _nbytes((tm, 4 * SSD_INNER + POOL_WIDTH + CONV_WIDTH + N_BRANCH * d), BF16)
              + 2 * _nbytes((HALO, POOL_WIDTH), BF16))
    est = 2 * stream + w_bytes + 10 * _nbytes((tm, d), F32)
    return pl.pallas_call(
        functools.partial(_merge_kernel, n_ctx=n_ctx),
        grid=(b, s // tm),
        in_specs=[pl.BlockSpec((1, tm, d), tile),
                  pl.BlockSpec((1, N_MOD, d), lambda i, t: (i, 0, 0)),
                  _resident(mc.shape),
                  pl.BlockSpec((1, tm, SSD_INNER), tile),
                  pl.BlockSpec((1, tm, SSD_INNER), tile),
                  pl.BlockSpec((1, tm, SSD_INNER), tile),
                  pl.BlockSpec((1, tm, MLA_HEADS * MLA_V), lambda i, t: (i, jnp.minimum(t, ctx_tiles - 1), 0)),
                  pl.BlockSpec((1, tm, MLA_HEADS * MLA_V), lambda i, t: (i, jnp.maximum(t - ctx_tiles, 0), 0)),
                  pl.BlockSpec((1, tm, POOL_WIDTH), tile),
                  pl.BlockSpec((1, HALO, POOL_WIDTH), prev),
                  pl.BlockSpec((1, HALO, POOL_WIDTH), nxt),
                  pl.BlockSpec((1, tm, CONV_WIDTH), tile),
                  pl.BlockSpec((1, tm, N_BRANCH * d), tile)]
                 + w_specs,
        out_specs=pl.BlockSpec((1, tm, d), tile),
        out_shape=jax.ShapeDtypeStruct(h.shape, F32),
        compiler_params=_params(est, 2),
        name="merge",
    )(h, ml, mc, yf, yb, zs, att_c, att_l, pv, pv, pv, cv, sg, *[x for x, _ in weights])


def _rope_tables(n_ctx, n_lat):
    rows = n_lat // GRID_W
    row = np.repeat(np.arange(rows), GRID_W).astype(np.float32)
    col = np.tile(np.arange(GRID_W), rows).astype(np.float32)
    half = MLA_ROPE // 2
    inv = (1.0 / (ROPE_THETA ** (jnp.arange(0, half, 2, dtype=F32) / half)))
    ar = jnp.asarray(row)[:, None] * inv
    ac = jnp.asarray(col)[:, None] * inv
    ang = jnp.concatenate([ar, ar, ac, ac], axis=-1)
    pad = LANES - MLA_ROPE
    cos = jnp.concatenate([jnp.ones((n_ctx, MLA_ROPE), F32), jnp.cos(ang)], axis=0)
    sin = jnp.concatenate([jnp.zeros((n_ctx, MLA_ROPE), F32), jnp.sin(ang)], axis=0)
    return jnp.pad(cos, ((0, 0), (0, pad))), jnp.pad(sin, ((0, 0), (0, pad)))


def _rot_cols(w):
    w1, w2, w3, w4 = jnp.split(w, 4, axis=-1)
    return jnp.concatenate([-w2, w1, -w4, w3], axis=-1)


def _layer_weights(i, p):
    w_in = p["w_in"][i]
    splits = np.cumsum([SSD_INNER, SSD_XBC, 2 * SSD_HEADS, MLA_Q_RANK, MLA_KV_RANK, MLA_ROPE,
                        POOL_WIDTH, 3 * CONV_WIDTH])
    wz, wxbc, wdt, wqd, wkvd, wkr, wpv, wcv, wg = jnp.split(w_in, splits, axis=1)
    lane_pad = lambda x: jnp.pad(x, ((0, 0), (0, LANES - x.shape[1])))
    uq = p["mla_w_uq"][i].reshape(MLA_Q_RANK, MLA_HEADS, MLA_QK)
    uq_pe = uq[:, :, MLA_NOPE:]
    head_pad = lambda x: jnp.pad(x, ((0, 0), (0, 0), (0, LANES - MLA_ROPE))).reshape(MLA_Q_RANK, MLA_HEADS * LANES)
    bf = lambda x: x.astype(BF16)
    return {
        "wzx": bf(w_in[:, :SSD_INNER + SSD_XBC]), "wloc": bf(w_in[:, splits[5]:]),
        "wsm": bf(jnp.concatenate([wqd, wkvd, lane_pad(wkr), lane_pad(_rot_cols(wkr)), lane_pad(wdt)], axis=1)),
        "q_norm": p["mla_q_norm"][i].reshape(1, -1), "kv_norm": p["mla_kv_norm"][i].reshape(1, -1),
        "wuqn": bf(uq[:, :, :MLA_NOPE].reshape(MLA_Q_RANK, MLA_HEADS * MLA_NOPE)),
        "wuqp": bf(head_pad(uq_pe)), "wuqr": bf(head_pad(_rot_cols(uq_pe))),
        "wuk": bf(p["mla_w_uk"][i]), "wuvt": bf(p["mla_w_uv"][i].T),
        "ssd_norm": p["ssd_norm"][i].reshape(1, -1), "pool_scale": p["pool_scale"][i].reshape(1, -1),
        "sconv_w": p["sconv_w"][i],
    }


def kernel(x, c, ctx, c_ctx, w_ada, b_ada, ffn1_norm, ffn1_w_gate, ffn1_w_up, ffn1_w_down, mix_norm, w_in, ssd_conv_w, ssd_conv_b, ssd_dt_bias, ssd_a_log, ssd_d, ssd_norm, ssd_w_out, mla_q_norm, mla_w_uq, mla_kv_norm, mla_w_uk, mla_w_uv, mla_w_out, pool_w, pool_scale, pool_w_out, sconv_w, sconv_w_out, w_o, ffn2_norm, ffn2_w_gate, ffn2_w_up, ffn2_w_down, final_norm):
    p = dict(w_in=w_in.astype(BF16), ssd_norm=ssd_norm, mla_q_norm=mla_q_norm, mla_w_uq=mla_w_uq,
             mla_kv_norm=mla_kv_norm, mla_w_uk=mla_w_uk, mla_w_uv=mla_w_uv, pool_scale=pool_scale, sconv_w=sconv_w)
    stacked = {name: a.astype(BF16) for name, a in dict(
        ssd_w_out=ssd_w_out, mla_w_out=mla_w_out, pool_w=pool_w, pool_w_out=pool_w_out,
        sconv_w_out=sconv_w_out, w_o=w_o).items()}
    bsz, n_lat, d = x.shape
    n_ctx = ctx.shape[1]
    depth = w_ada.shape[0]

    mod_rows = -(-(bsz + 1) // 8) * 8
    cc = jnp.concatenate([c, c_ctx[None, :], jnp.zeros((mod_rows - bsz - 1, d), F32)], axis=0)
    mods = _adaln(cc, w_ada, b_ada).reshape(depth, mod_rows, N_MOD, d)

    cos_t, sin_t = _rope_tables(n_ctx, n_lat)
    ffn1_w = [a.astype(BF16) for a in (ffn1_w_gate, ffn1_w_up, ffn1_w_down)]
    ffn2_w = [a.astype(BF16) for a in (ffn2_w_gate, ffn2_w_up, ffn2_w_down)]
    h = x
    for i in range(depth):
        ml, mc = mods[i, :bsz], mods[i, bsz]
        w = _layer_weights(i, p)
        h = _ffn(h, ml, mc, ffn1_norm[i], *ffn1_w, i, n_ctx=n_ctx, mod0=0, ctx=ctx if i == 0 else None)
        zs, xs, cm, bt, dt, qn, qp, kn, kp, vt = _proj_seq(
            h, ml, mc, mix_norm[i], cos_t, sin_t, ssd_conv_w[i], ssd_conv_b[i], w, n_ctx=n_ctx)
        pv, cv, sg = _proj_loc(h, ml, mc, mix_norm[i], sconv_w[i], w, n_ctx=n_ctx)
        yf, yb = _ssd_scan(xs, cm, bt, dt, ssd_dt_bias[i], ssd_a_log[i], ssd_d[i], n_ctx=n_ctx)
        att_c, att_l = _attention(qn, qp, kn, kp, vt, n_ctx=n_ctx)
        h = _merge(h, ml, mc, yf, yb, zs, att_c, att_l, pv, cv, sg, w, stacked, i, n_ctx=n_ctx)
        h = _ffn(h, ml, mc, ffn2_norm[i], *ffn2_w, i, n_ctx=n_ctx, mod0=6,
                 final_w=final_norm if i == depth - 1 else None)
    return h
```
